```python
import math
import jax, jax.numpy as jnp
from jax import lax
import numpy as np

D_MODEL = 2048
BATCH = 4
SEQ = 2048
DEPTH = 4
DEC_BATCH = 8
DEC_SEQ = 8
PAST_LEN = 16384
PAGE_SIZE = 128

N_EVEN = (DEPTH + 1) // 2
N_ODD = DEPTH // 2
A_WIDTH = D_MODEL // 2
SCONV_W = 3
HEAD_DIM = 64
N_HEADS = (D_MODEL // 2) // HEAD_DIM
KV_HEADS = 4
GQA = N_HEADS // KV_HEADS
NSA_BLOCK = 64
N_SEL = 16
WINDOW = 512
Q_BLOCK = 64
ROPE_THETA = 10000.0
C_WIDTH = D_MODEL // 2
CCONV_W = 31
D_WIDTH = D_MODEL // 2
D_CHUNK = 128
D_GROUPS = 4
D_FF = 5632
FFN_CONV_W = 3
EPS = 1e-6
NEG = -1e30

Q_WIDTH = N_HEADS * HEAD_DIM
KV_WIDTH = 6 * KV_HEADS * HEAD_DIM
IN_EVEN = 3 * A_WIDTH + Q_WIDTH + KV_WIDTH + 3 * N_HEADS
MIX_EVEN = A_WIDTH + Q_WIDTH
IN_ODD = 2 * C_WIDTH + 2 * D_WIDTH
MIX_ODD = C_WIDTH + D_WIDTH

kernel_name = 'hybrid_conv_nsa_conformer_gmlp_decode_step'


def rmsnorm(x, g):
    x32 = x.astype(jnp.float32)
    y = x32 * lax.rsqrt(jnp.mean(x32 * x32, axis=-1, keepdims=True) + EPS)
    return (y * g.astype(jnp.float32)).astype(x.dtype)


def layernorm(x, g, b):
    x32 = x.astype(jnp.float32)
    mu = jnp.mean(x32, axis=-1, keepdims=True)
    xc = x32 - mu
    y = xc * lax.rsqrt(jnp.mean(xc * xc, axis=-1, keepdims=True) + EPS)
    return (y * g.astype(jnp.float32) + b.astype(jnp.float32)).astype(x.dtype)


def rope(x, pos):
    half = x.shape[-1] // 2
    freq = ROPE_THETA ** (-jnp.arange(half, dtype=jnp.float32) / half)
    ang = pos.astype(jnp.float32)[:, None] * freq[None, :]
    shape = (1, pos.shape[0]) + (1,) * (x.ndim - 3) + (half,)
    cos, sin = jnp.cos(ang).reshape(shape), jnp.sin(ang).reshape(shape)
    x32 = x.astype(jnp.float32)
    x1, x2 = x32[..., :half], x32[..., half:]
    return jnp.concatenate([x1 * cos - x2 * sin, x2 * cos + x1 * sin], axis=-1).astype(x.dtype)


def causal_dwconv(u, w, buf):
    full = jnp.concatenate([buf.astype(u.dtype), u], axis=1)
    k = w.shape[0]
    y = lax.conv_general_dilated(full, w[:, None, :].astype(u.dtype), window_strides=(1,),
                                 padding='VALID', dimension_numbers=('NWC', 'WIO', 'NWC'),
                                 feature_group_count=u.shape[-1])
    return y, full[:, full.shape[1] - (k - 1):]


def _nsa_blocks(rows):
    b, t = rows.shape[:2]
    blocks = rows.reshape(b, t // NSA_BLOCK, NSA_BLOCK, 4, KV_HEADS, HEAD_DIM)
    cmp_mean = jnp.mean(blocks[:, :, :, 0:2], axis=2)
    return cmp_mean[:, :, 0], cmp_mean[:, :, 1], blocks[:, :, :, 2], blocks[:, :, :, 3]


def _nsa_attend(q, qpos, kc, vc, ks, vs, kw, vw, kwpos, gates):
    b, nq = q.shape[0], q.shape[1]
    nb = kc.shape[1]
    scale = HEAD_DIM ** -0.5
    qg = q.reshape(b, nq, KV_HEADS, GQA, HEAD_DIM)
    blk = jnp.arange(nb)
    s = jnp.einsum('bqgrd,bngd->bqgrn', qg, kc).astype(jnp.float32) * scale
    ok_c = ((blk[None, :] + 1) * NSA_BLOCK <= qpos[:, None] + 1)[None, :, None, None, :]
    p_c = jnp.where(ok_c, jax.nn.softmax(jnp.where(ok_c, s, NEG), axis=-1), 0.0)
    o_cmp = jnp.einsum('bqgrn,bngd->bqgrd', p_c.astype(vc.dtype), vc)
    cur = qpos // NSA_BLOCK
    forced = (blk[None, :] == 0) | (blk[None, :] == cur[:, None]) | (blk[None, :] == cur[:, None] - 1)
    causal_blk = blk[None, :] <= cur[:, None]
    imp = jnp.sum(p_c, axis=3)
    imp = jnp.where(forced[None, :, None, :], GQA + 1.0, imp)
    imp = jnp.where(causal_blk[None, :, None, :], imp, -1.0)
    n_sel = min(N_SEL, nb)
    _, idx = lax.top_k(imp, n_sel)
    idx = idx.transpose(0, 2, 1, 3)
    take = jax.vmap(jax.vmap(lambda t, i: t[i]))
    kg = take(ks.transpose(0, 3, 1, 2, 4), idx)
    vg = take(vs.transpose(0, 3, 1, 2, 4), idx)
    kpos = idx[..., None] * NSA_BLOCK + jnp.arange(NSA_BLOCK)
    ok_s = (kpos <= qpos[None, None, :, None, None])[:, :, :, None]
    qt = qg.transpose(0, 2, 1, 3, 4)
    s = jnp.einsum('bgqrd,bgqnsd->bgqrns', qt, kg).astype(jnp.float32) * scale
    s = jnp.where(ok_s, s, NEG)
    p_s = jax.nn.softmax(s.reshape(s.shape[:4] + (n_sel * NSA_BLOCK,)), axis=-1).reshape(s.shape)
    o_sel = jnp.einsum('bgqrns,bgqnsd->bqgrd', p_s.astype(vg.dtype), vg)
    s = jnp.einsum('bqgrd,btgd->bqgrt', qg, kw).astype(jnp.float32) * scale
    ok_w = ((kwpos[None, :] <= qpos[:, None]) & (kwpos[None, :] > qpos[:, None] - WINDOW)
            & (kwpos[None, :] >= 0))[None, :, None, None, :]
    p_w = jax.nn.softmax(jnp.where(ok_w, s, NEG), axis=-1)
    o_win = jnp.einsum('bqgrt,btgd->bqgrd', p_w.astype(vw.dtype), vw)
    o = jnp.stack([o_cmp, o_sel, o_win], axis=-1).reshape(b, nq, N_HEADS, HEAD_DIM, 3)
    o = jnp.sum(o * gates[:, :, :, None, :].astype(o.dtype), axis=-1)
    return o.reshape(b, nq, Q_WIDTH)


def nsa_prompt(q, nsa_rows, win_rows, gates):
    b, s_len = q.shape[:2]
    kc, vc, ks, vs = _nsa_blocks(nsa_rows)
    pad = jnp.zeros((b, WINDOW) + win_rows.shape[2:], win_rows.dtype)
    wpad = jnp.concatenate([pad, win_rows], axis=1)
    nqb = s_len // Q_BLOCK
    qb = q.reshape(b, nqb, Q_BLOCK, N_HEADS, HEAD_DIM).swapaxes(0, 1)
    gb = gates.reshape(b, nqb, Q_BLOCK, N_HEADS, 3).swapaxes(0, 1)

    def one_block(args):
        i, q_i, g_i = args
        start = i * Q_BLOCK
        qpos = start + jnp.arange(Q_BLOCK)
        band = lax.dynamic_slice_in_dim(wpad, start, WINDOW + Q_BLOCK, axis=1)
        kwpos = start - WINDOW + jnp.arange(WINDOW + Q_BLOCK)
        return _nsa_attend(q_i, qpos, kc, vc, ks, vs, band[:, :, 0], band[:, :, 1], kwpos, g_i)

    o = lax.map(one_block, (jnp.arange(nqb), qb, gb))
    o = o.swapaxes(0, 1).reshape(b, s_len, Q_WIDTH)
    keep = min(WINDOW, s_len)
    return o, win_rows[:, s_len - keep:]


def nsa_sample(q, nsa_rows, win_rows, gates, pool, page_table, win_buf):
    b, s_new = q.shape[:2]
    past_len = page_table.shape[1] * PAGE_SIZE
    past = pool[page_table].reshape(b, past_len, 4, KV_HEADS, HEAD_DIM)
    full = jnp.concatenate([past, nsa_rows.astype(past.dtype)], axis=1)
    t_len = past_len + s_new
    nb = -(-t_len // NSA_BLOCK)
    full = jnp.pad(full, ((0, 0), (0, nb * NSA_BLOCK - t_len), (0, 0), (0, 0), (0, 0)))
    kc, vc, ks, vs = _nsa_blocks(full)
    wfull = jnp.concatenate([win_buf.astype(win_rows.dtype), win_rows], axis=1)
    wlen = win_buf.shape[1]
    kwpos = past_len - wlen + jnp.arange(wlen + s_new)
    qpos = past_len + jnp.arange(s_new)
    o = _nsa_attend(q, qpos, kc, vc, ks, vs, wfull[:, :, 0], wfull[:, :, 1], kwpos, gates)
    return o, wfull[:, s_new:]


def _even_project(h, pos, w_in):
    b, s_len, _ = h.shape
    z = h @ w_in
    cuts = [A_WIDTH, 2 * A_WIDTH, 3 * A_WIDTH, 3 * A_WIDTH + Q_WIDTH, 3 * A_WIDTH + Q_WIDTH + KV_WIDTH]
    a_in, a_b, a_c, q, kv, g = jnp.split(z, cuts, axis=-1)
    q = rope(q.reshape(b, s_len, N_HEADS, HEAD_DIM), pos)
    kv = kv.reshape(b, s_len, 3, 2, KV_HEADS, HEAD_DIM)
    k = rope(kv[:, :, :, 0], pos)
    v = kv[:, :, :, 1]
    nsa_rows = jnp.stack([k[:, :, 0], v[:, :, 0], k[:, :, 1], v[:, :, 1]], axis=2)
    win_rows = jnp.stack([k[:, :, 2], v[:, :, 2]], axis=2)
    gates = jax.nn.sigmoid(g.reshape(b, s_len, N_HEADS, 3))
    return a_c * a_in, a_b, q, nsa_rows, win_rows, gates


def chunk_spatial(v, ws, bs):
    b, s_len, c = v.shape
    n_ch = -(-s_len // D_CHUNK)
    vp = jnp.pad(v, ((0, 0), (0, n_ch * D_CHUNK - s_len), (0, 0)))
    vp = vp.reshape(b, n_ch, D_CHUNK, D_GROUPS, c // D_GROUPS)
    mask = jnp.tril(jnp.ones((D_CHUNK, D_CHUNK), dtype=bool))
    w = jnp.where(mask, ws, 0.0).astype(v.dtype)
    z = jnp.einsum('gts,bcsgd->bctgd', w, vp) + bs.T[None, None, :, :, None].astype(v.dtype)
    return z.reshape(b, n_ch * D_CHUNK, c)[:, :s_len]


def odd_mixer(h, cbuf, w_in, cconv_w, cconv_b, c_ln_g, c_ln_b, d_ln_g, d_ln_b, d_ws, d_bs, w_out):
    z = h @ w_in
    c_a, c_g, d_z = jnp.split(z, [C_WIDTH, 2 * C_WIDTH], axis=-1)
    c = c_a * jax.nn.sigmoid(c_g)
    c, cbuf_new = causal_dwconv(c, cconv_w, cbuf)
    c = jax.nn.silu(layernorm(c + cconv_b.astype(c.dtype), c_ln_g, c_ln_b))
    d_z = jax.nn.gelu(d_z)
    u, v = d_z[..., :D_WIDTH], d_z[..., D_WIDTH:]
    v = layernorm(v, d_ln_g, d_ln_b)
    d = u * chunk_spatial(v, d_ws, d_bs)
    return jnp.concatenate([c, d], axis=-1) @ w_out, cbuf_new, v


def conv_ffn(h, w_up, conv_w, w_down, buf):
    a, g = jnp.split(h @ w_up, 2, axis=-1)
    a, buf_new = causal_dwconv(a, conv_w, buf)
    return (jax.nn.silu(a) * g) @ w_down, buf_new


def _run_trunk(x, c, pos, sconv_bufs, cconv_bufs, ffn_bufs, nsa_call,
               g_mix, g_ffn, g_final, w_ada, b_ada, w_in_even, sconv_w, w_out_even,
               w_in_odd, cconv_w, cconv_b, c_ln_g, c_ln_b, d_ln_g, d_ln_b, d_ws, d_bs,
               w_out_odd, w_up, ffn_conv_w, w_down):
    c_act = jax.nn.silu(c)
    new_nsa, new_win, new_s, new_c, new_dv, new_f = [], [], [], [], [], []
    for l in range(DEPTH):
        i = l // 2
        mod = c_act @ w_ada[l] + b_ada[l]
        sh_m, sc_m, gt_m, sh_f, sc_f, gt_f = [m[:, None, :] for m in jnp.split(mod, 6, axis=-1)]
        h = rmsnorm(x, g_mix[l]) * (1 + sc_m) + sh_m
        if l % 2 == 0:
            u, a_b, q, nsa_rows, win_rows, gates = _even_project(h, pos, w_in_even[i])
            conv, sb = causal_dwconv(u, sconv_w[i], sconv_bufs[i])
            o_b, wst = nsa_call(i, q, nsa_rows, win_rows, gates)
            y = jnp.concatenate([a_b * conv, o_b], axis=-1) @ w_out_even[i]
            new_s.append(sb)
            new_nsa.append(nsa_rows)
            new_win.append(wst)
        else:
            y, cb, v = odd_mixer(h, cconv_bufs[i], w_in_odd[i], cconv_w[i], cconv_b[i], c_ln_g[i],
                                 c_ln_b[i], d_ln_g[i], d_ln_b[i], d_ws[i], d_bs[i], w_out_odd[i])
            new_c.append(cb)
            new_dv.append(v)
        x = x + gt_m * y
        h = rmsnorm(x, g_ffn[l]) * (1 + sc_f) + sh_f
        f, fb = conv_ffn(h, w_up[l], ffn_conv_w[l], w_down[l], ffn_bufs[l])
        new_f.append(fb)
        x = x + gt_f * f
    return (rmsnorm(x, g_final), jnp.stack(new_nsa), jnp.stack(new_win), jnp.stack(new_s),
            jnp.stack(new_c), jnp.stack(new_dv), jnp.stack(new_f))


def setup_inputs(seed: int = 0) -> dict:
    key = jax.random.key(seed)
    ks = jax.random.split(key, 32)

    def nrm(i, shape, s):
        return jax.random.normal(ks[i], shape, jnp.float32) * s

    n_pages = PAST_LEN // PAGE_SIZE
    used = DEC_BATCH * n_pages
    pool = used + max(1, used // 4)
    wbuf = min(WINDOW, PAST_LEN)
    page_table = jax.random.permutation(ks[7], pool)[:used].reshape(DEC_BATCH, n_pages).astype(jnp.int32)
    row_scale = (jnp.arange(D_CHUNK, dtype=jnp.float32) + 1.0) ** -0.5
    return {
        'x_prompt': nrm(0, (BATCH, SEQ, D_MODEL), 1.0),
        'x_sample': nrm(1, (DEC_BATCH, DEC_SEQ, D_MODEL), 1.0),
        'cache_nsa_kv': nrm(2, (N_EVEN, pool, PAGE_SIZE, 4, KV_HEADS, HEAD_DIM), 1.0),
        'state_win_kv': nrm(3, (N_EVEN, DEC_BATCH, wbuf, 2, KV_HEADS, HEAD_DIM), 1.0),
        'state_sconv': nrm(4, (N_EVEN, DEC_BATCH, SCONV_W - 1, A_WIDTH), 1.0),
        'state_cconv': nrm(5, (N_ODD, DEC_BATCH, CCONV_W - 1, C_WIDTH), 0.5),
        'state_ffn_conv': nrm(6, (DEPTH, DEC_BATCH, FFN_CONV_W - 1, D_FF), 1.0),
        'page_table': page_table,
        'c_prompt': nrm(8, (BATCH, D_MODEL), 1.0),
        'c_sample': nrm(9, (DEC_BATCH, D_MODEL), 1.0),
        'g_mix': 1.0 + nrm(10, (DEPTH, D_MODEL), 0.05),
        'g_ffn': 1.0 + nrm(11, (DEPTH, D_MODEL), 0.05),
        'g_final': 1.0 + nrm(12, (D_MODEL,), 0.05),
        'w_ada': nrm(13, (DEPTH, D_MODEL, 6 * D_MODEL), 0.5 * D_MODEL ** -0.5),
        'b_ada': nrm(14, (DEPTH, 6 * D_MODEL), 0.02),
        'w_in_even': nrm(15, (N_EVEN, D_MODEL, IN_EVEN), D_MODEL ** -0.5),
        'sconv_w': nrm(16, (N_EVEN, SCONV_W, A_WIDTH), SCONV_W ** -0.5),
        'w_out_even': nrm(17, (N_EVEN, MIX_EVEN, D_MODEL), MIX_EVEN ** -0.5),
        'w_in_odd': nrm(18, (N_ODD, D_MODEL, IN_ODD), D_MODEL ** -0.5),
        'cconv_w': nrm(19, (N_ODD, CCONV_W, C_WIDTH), CCONV_W ** -0.5),
        'cconv_b': nrm(20, (N_ODD, C_WIDTH), 0.02),
        'c_ln_g': 1.0 + nrm(21, (N_ODD, C_WIDTH), 0.05),
        'c_ln_b': nrm(22, (N_ODD, C_WIDTH), 0.02),
        'd_ln_g': 1.0 + nrm(23, (N_ODD, D_WIDTH), 0.05),
        'd_ln_b': nrm(24, (N_ODD, D_WIDTH), 0.02),
        'd_ws': nrm(25, (N_ODD, D_GROUPS, D_CHUNK, D_CHUNK), 1.0) * row_scale[:, None],
        'd_bs': nrm(26, (N_ODD, D_GROUPS, D_CHUNK), 0.02),
        'w_out_odd': nrm(27, (N_ODD, MIX_ODD, D_MODEL), MIX_ODD ** -0.5),
        'w_up': nrm(28, (DEPTH, D_MODEL, 2 * D_FF), D_MODEL ** -0.5),
        'ffn_conv_w': nrm(29, (DEPTH, FFN_CONV_W, D_FF), FFN_CONV_W ** -0.5),
        'w_down': nrm(30, (DEPTH, D_FF, D_MODEL), D_FF ** -0.5),
    }


def reference(x_prompt, x_sample, cache_nsa_kv, state_win_kv, state_sconv, state_cconv,
              state_ffn_conv, page_table, c_prompt, c_sample, g_mix, g_ffn, g_final, w_ada,
              b_ada, w_in_even, sconv_w, w_out_even, w_in_odd, cconv_w, cconv_b, c_ln_g,
              c_ln_b, d_ln_g, d_ln_b, d_ws, d_bs, w_out_odd, w_up, ffn_conv_w, w_down):
    weights = (g_mix, g_ffn, g_final, w_ada, b_ada, w_in_even, sconv_w, w_out_even,
               w_in_odd, cconv_w, cconv_b, c_ln_g, c_ln_b, d_ln_g, d_ln_b, d_ws, d_bs,
               w_out_odd, w_up, ffn_conv_w, w_down)
    bp, sp = x_prompt.shape[0], x_prompt.shape[1]
    ss = x_sample.shape[1]
    past_len = page_table.shape[1] * PAGE_SIZE
    dt = x_prompt.dtype
    pos_p = jnp.arange(sp, dtype=jnp.int32)
    pos_s = past_len + jnp.arange(ss, dtype=jnp.int32)

    zs = jnp.zeros((N_EVEN, bp, SCONV_W - 1, A_WIDTH), dt)
    zc = jnp.zeros((N_ODD, bp, CCONV_W - 1, C_WIDTH), dt)
    zf = jnp.zeros((DEPTH, bp, FFN_CONV_W - 1, D_FF), dt)
    prompt_nsa = lambda i, q, rows, wrows, gates: nsa_prompt(q, rows, wrows, gates)
    (y_prompt, nsa_kv_prompt, win_kv_prompt, sconv_prompt, cconv_prompt, _,
     ffn_prompt) = _run_trunk(x_prompt, c_prompt, pos_p, zs, zc, zf, prompt_nsa, *weights)

    sample_nsa = lambda i, q, rows, wrows, gates: nsa_sample(
        q, rows, wrows, gates, cache_nsa_kv[i], page_table, state_win_kv[i])
    (y_sample, nsa_kv_sample, win_kv_sample, sconv_sample, cconv_sample, dv_sample,
     ffn_sample) = _run_trunk(x_sample, c_sample, pos_s, state_sconv, state_cconv,
                              state_ffn_conv, sample_nsa, *weights)

    return (y_prompt, y_sample, nsa_kv_prompt, nsa_kv_sample, win_kv_prompt, win_kv_sample,
            sconv_prompt, sconv_sample, cconv_prompt, cconv_sample, dv_sample, ffn_prompt,
            ffn_sample)
```

```python
import functools
import math

import jax
import jax.numpy as jnp
from jax import lax
from jax.experimental import pallas as pl
from jax.experimental.pallas import tpu as pltpu

BF = jnp.bfloat16
F32 = jnp.float32

HEAD_DIM = 64
N_HEADS = 16
KV_HEADS = 4
GQA = N_HEADS // KV_HEADS
NSA_BLOCK = 64
N_SEL = 16
WINDOW = 512
PAGE_SIZE = 128
ROPE_THETA = 10000.0
CCONV_W = 31
D_CHUNK = 128
D_GROUPS = 4
EPS = 1e-6
NEG = -1e30

LANES = 128
VMEM_LIMIT = 56 * 1024 * 1024


def _params(*sem):
    return pltpu.CompilerParams(dimension_semantics=sem, vmem_limit_bytes=VMEM_LIMIT)


def _sigmoid(x):
    return 1.0 / (1.0 + jnp.exp(-x))


def _silu(x):
    return x * _sigmoid(x)


def _gelu_tanh(x):
    return 0.5 * x * (1.0 + jnp.tanh(math.sqrt(2.0 / math.pi) * (x + 0.044715 * (x * x * x))))


def _layernorm(x, g, b):
    mu = jnp.mean(x, axis=-1, keepdims=True)
    xc = x - mu
    return xc * lax.rsqrt(jnp.mean(xc * xc, axis=-1, keepdims=True) + EPS) * g + b


def _ada_kernel(c_ref, w_ref, b_ref, o_ref):
    ca = _silu(c_ref[...]).astype(BF)
    acc = jnp.dot(ca, w_ref[...].astype(BF), preferred_element_type=F32)
    o_ref[...] = acc + b_ref[...]


def _ada(c16, w_ada, b_ada):
    depth, d, n6 = w_ada.shape
    rows = c16.shape[0]
    tn = 1024
    per = d // tn
    return pl.pallas_call(
        _ada_kernel,
        grid=(depth, n6 // tn),
        in_specs=[
            pl.BlockSpec((rows, d), lambda l, j: (0, 0)),
            pl.BlockSpec((None, d, tn), lambda l, j: (l, 0, j)),
            pl.BlockSpec((None, 1, tn), lambda l, j: (l, 0, j)),
        ],
        out_specs=pl.BlockSpec((None, None, rows, tn), lambda l, j: (l, j // per, 0, j % per)),
        out_shape=jax.ShapeDtypeStruct((depth, 6, rows, d), F32),
        compiler_params=_params("arbitrary", "arbitrary"),
        name="ada",
    )(c16, w_ada, b_ada.reshape(depth, 1, n6))


def _norm_mod_kernel(x_ref, g_ref, sh_ref, sc_ref, o_ref):
    x = x_ref[...]
    y = x * lax.rsqrt(jnp.mean(x * x, axis=-1, keepdims=True) + EPS) * g_ref[...]
    o_ref[...] = (y * (1.0 + sc_ref[...]) + sh_ref[...]).astype(o_ref.dtype)


def _norm_mod(x, g, mod5, l, which, b_off, tm, tpb):
    m, d = x.shape
    mod_spec = lambda w: pl.BlockSpec((None, None, None, 1, d),
                                      lambda i: (l, w, b_off + i // tpb, 0, 0))
    return pl.pallas_call(
        _norm_mod_kernel,
        grid=(m // tm,),
        in_specs=[
            pl.BlockSpec((tm, d), lambda i: (i, 0)),
            pl.BlockSpec((None, 1, d), lambda i: (l, 0, 0)),
            mod_spec(which), mod_spec(which + 1),
        ],
        out_specs=pl.BlockSpec((tm, d), lambda i: (i, 0)),
        out_shape=jax.ShapeDtypeStruct((m, d), BF),
        compiler_params=_params("arbitrary"),
        name="norm_mod",
    )(x, g.reshape(g.shape[0], 1, d), mod5, mod5)


def _final_norm_kernel(x_ref, g_ref, o_ref):
    x = x_ref[...]
    o_ref[...] = x * lax.rsqrt(jnp.mean(x * x, axis=-1, keepdims=True) + EPS) * g_ref[...]


def _final_norm(x, g, tm):
    m, d = x.shape
    return pl.pallas_call(
        _final_norm_kernel,
        grid=(m // tm,),
        in_specs=[pl.BlockSpec((tm, d), lambda i: (i, 0)), pl.BlockSpec((1, d), lambda i: (0, 0))],
        out_specs=pl.BlockSpec((tm, d), lambda i: (i, 0)),
        out_shape=jax.ShapeDtypeStruct((m, d), F32),
        compiler_params=_params("arbitrary"),
        name="final_norm",
    )(x, g.reshape(1, d))


def _mm_kernel(*refs, k_sizes, res_gate, wt):
    n_a = len(k_sizes)
    a_refs = refs[:n_a]
    w_ref = refs[n_a]
    pos = n_a + 1
    if res_gate:
        res_ref, gate_ref = refs[pos], refs[pos + 1]
        pos += 2
    o_ref, wb_ref = refs[pos], refs[pos + 1]

    @pl.when(pl.program_id(1) == 0)
    def _():
        wb_ref[...] = w_ref[...].astype(BF)

    acc = None
    k0 = 0
    for a_ref, ks in zip(a_refs, k_sizes):
        if wt:
            part = lax.dot_general(a_ref[...], wb_ref[:, k0:k0 + ks], (((1,), (1,)), ((), ())),
                                   preferred_element_type=F32)
        else:
            part = jnp.dot(a_ref[...], wb_ref[k0:k0 + ks, :], preferred_element_type=F32)
        acc = part if acc is None else acc + part
        k0 += ks
    if res_gate:
        acc = res_ref[...] + gate_ref[...] * acc
    o_ref[...] = acc.astype(o_ref.dtype)


def _mm(a_list, w, l, n, tm, tn, wt=False, res=None, gate_spec=None, gate=None, name="mm"):
    m = a_list[0].shape[0]
    k_sizes = tuple(a.shape[1] for a in a_list)
    k = sum(k_sizes)
    assert w.shape[2 if wt else 1] == k and n % tn == 0 and m % tm == 0
    in_specs = [pl.BlockSpec((tm, ks), lambda j, i: (i, 0)) for ks in k_sizes]
    if wt:
        in_specs.append(pl.BlockSpec((None, tn, k), lambda j, i: (l, j, 0)))
    else:
        in_specs.append(pl.BlockSpec((None, k, tn), lambda j, i: (l, 0, j)))
    args = list(a_list) + [w]
    if res is not None:
        in_specs += [pl.BlockSpec((tm, tn), lambda j, i: (i, j)), gate_spec]
        args += [res, gate]
    return pl.pallas_call(
        functools.partial(_mm_kernel, k_sizes=k_sizes, res_gate=res is not None, wt=wt),
        grid=(n // tn, m // tm),
        in_specs=in_specs,
        out_specs=pl.BlockSpec((tm, tn), lambda j, i: (i, j)),
        out_shape=jax.ShapeDtypeStruct((m, n), F32),
        scratch_shapes=[pltpu.VMEM((tn, k) if wt else (k, tn), BF)],
        compiler_params=_params("arbitrary", "arbitrary"),
        name=name,
    )(*args)


def _conv3(u, p, w):
    row = lax.broadcasted_iota(jnp.int32, u.shape, 0)
    um1 = jnp.where(row == 0, p[1:2], pltpu.roll(u, 1, 0))
    um2 = jnp.where(row == 0, p[0:1], jnp.where(row == 1, p[1:2], pltpu.roll(u, 2, 0)))
    return w[0:1] * um2 + w[1:2] * um1 + w[2:3] * u


def _prev_rows(i, tpb, prev_ref, carry_ref):
    if tpb == 1:
        return prev_ref[...]
    return jnp.where(i % tpb == 0, prev_ref[...], carry_ref[6:8, :])


def _mixer_a_kernel(ain_ref, ab_ref, ac_ref, w_ref, prev_ref, o_ref, st_ref, carry_ref, *, tm, tpb):
    i = pl.program_id(0)
    u = ac_ref[...] * ain_ref[...]
    p = _prev_rows(i, tpb, prev_ref, carry_ref)
    o_ref[...] = (ab_ref[...] * _conv3(u, p, w_ref[...])).astype(o_ref.dtype)
    st_ref[...] = u[tm - 2:tm]
    if tpb > 1:
        carry_ref[...] = u[tm - 8:tm]


def _mixer_a(z, sconv_w, l, prev, tm, tpb):
    m = z.shape[0]
    c = prev.shape[-1]
    nb = prev.shape[0]
    return pl.pallas_call(
        functools.partial(_mixer_a_kernel, tm=tm, tpb=tpb),
        grid=(m // tm,),
        in_specs=[
            pl.BlockSpec((tm, c), lambda i: (i, 0)),
            pl.BlockSpec((tm, c), lambda i: (i, 1)),
            pl.BlockSpec((tm, c), lambda i: (i, 2)),
            pl.BlockSpec((None, 3, c), lambda i: (l, 0, 0)),
            pl.BlockSpec((None, 2, c), lambda i: (i // tpb, 0, 0)),
        ],
        out_specs=[
            pl.BlockSpec((tm, c), lambda i: (i, 0)),
            pl.BlockSpec((None, 2, c), lambda i: (i // tpb, 0, 0)),
        ],
        out_shape=[jax.ShapeDtypeStruct((m, c), BF), jax.ShapeDtypeStruct((nb, 2, c), F32)],
        scratch_shapes=[pltpu.VMEM((8, c), F32)],
        compiler_params=_params("arbitrary"),
        name="mixer_a",
    )(z, z, z, sconv_w, prev)


def _ffn_act_kernel(a_ref, g_ref, w_ref, prev_ref, o_ref, st_ref, carry_ref, *, tm, tpb):
    i = pl.program_id(1)
    a = a_ref[...]
    p = _prev_rows(i, tpb, prev_ref, carry_ref)
    o_ref[...] = (_silu(_conv3(a, p, w_ref[...])) * g_ref[...]).astype(o_ref.dtype)
    st_ref[...] = a[tm - 2:tm]
    if tpb > 1:
        carry_ref[...] = a[tm - 8:tm]


def _ffn_act(zu, conv_w, l, prev, tm, tpb, tn):
    m = zu.shape[0]
    dff = prev.shape[-1]
    nb = prev.shape[0]
    nj = dff // tn
    return pl.pallas_call(
        functools.partial(_ffn_act_kernel, tm=tm, tpb=tpb),
        grid=(nj, m // tm),
        in_specs=[
            pl.BlockSpec((tm, tn), lambda j, i: (i, j)),
            pl.BlockSpec((tm, tn), lambda j, i: (i, j + nj)),
            pl.BlockSpec((None, 3, tn), lambda j, i: (l, 0, j)),
            pl.BlockSpec((None, 2, tn), lambda j, i: (i // tpb, 0, j)),
        ],
        out_specs=[
            pl.BlockSpec((tm, tn), lambda j, i: (i, j)),
            pl.BlockSpec((None, 2, tn), lambda j, i: (i // tpb, 0, j)),
        ],
        out_shape=[jax.ShapeDtypeStruct((m, dff), BF), jax.ShapeDtypeStruct((nb, 2, dff), F32)],
        scratch_shapes=[pltpu.VMEM((8, tn), F32)],
        compiler_params=_params("arbitrary", "arbitrary"),
        name="ffn_act",
    )(zu, zu, conv_w, prev)


def _odd_post_kernel(ca_ref, cg_ref, du_ref, dv_ref, cw_ref, cb_ref, clg_ref, clb_ref, dlg_ref,
                     dlb_ref, ws_ref, bst_ref, prev_ref, o_ref, st_ref, v_ref, cbuf, *, tm, tpb):
    i = pl.program_id(0)
    cw = cw_ref.shape[-1]
    c = ca_ref[...] * _sigmoid(cg_ref[...])
    if tpb == 1:
        cbuf[2:32, :] = prev_ref[...]
    else:
        @pl.when(i % tpb == 0)
        def _():
            cbuf[2:32, :] = prev_ref[...]

        @pl.when(i % tpb != 0)
        def _():
            cbuf[0:32, :] = cbuf[tm:tm + 32, :]
    cbuf[32:32 + tm, :] = c
    acc = cw_ref[0:1, :] * cbuf[2:2 + tm, :]
    for k in range(1, CCONV_W):
        acc = acc + cw_ref[k:k + 1, :] * cbuf[2 + k:2 + k + tm, :]
    st_ref[...] = cbuf[tm + 2:tm + 32, :]
    o_ref[:, 0:cw] = _silu(_layernorm(acc + cb_ref[...], clg_ref[...], clb_ref[...])).astype(o_ref.dtype)

    u = _gelu_tanh(du_ref[...])
    v = _layernorm(_gelu_tanh(dv_ref[...]), dlg_ref[...], dlb_ref[...])
    v_ref[...] = v
    gw = cw // D_GROUPS
    trow = lax.broadcasted_iota(jnp.int32, (D_CHUNK, D_CHUNK), 0)
    tcol = lax.broadcasted_iota(jnp.int32, (D_CHUNK, D_CHUNK), 1)
    rows = min(tm, D_CHUNK)
    for ch in range(max(1, tm // D_CHUNK)):
        r0 = ch * D_CHUNK
        vch = v[r0:r0 + rows]
        if rows < D_CHUNK:
            vch = jnp.concatenate([vch, jnp.zeros((D_CHUNK - rows, cw), F32)], axis=0)
        vch = vch.astype(BF)
        for g in range(D_GROUPS):
            wg = jnp.where(tcol <= trow, ws_ref[g], 0.0).astype(BF)
            zz = jnp.dot(wg, vch[:, g * gw:(g + 1) * gw], preferred_element_type=F32)
            zz = zz + bst_ref[:, g:g + 1]
            o_ref[r0:r0 + rows, cw + g * gw:cw + (g + 1) * gw] = (
                u[r0:r0 + rows, g * gw:(g + 1) * gw] * zz[0:rows]).astype(o_ref.dtype)


def _odd_post(z, i_odd, prev, cconv_w, cconv_b, c_ln_g, c_ln_b, d_ln_g, d_ln_b, d_ws, d_bs_t, tm, tpb):
    m = z.shape[0]
    c = prev.shape[-1]
    nb = prev.shape[0]
    vec = lambda: pl.BlockSpec((None, 1, c), lambda i: (i_odd, 0, 0))
    r3 = lambda a: a.reshape(a.shape[0], 1, c)
    return pl.pallas_call(
        functools.partial(_odd_post_kernel, tm=tm, tpb=tpb),
        grid=(m // tm,),
        in_specs=[
            pl.BlockSpec((tm, c), lambda i: (i, 0)),
            pl.BlockSpec((tm, c), lambda i: (i, 1)),
            pl.BlockSpec((tm, c), lambda i: (i, 2)),
            pl.BlockSpec((tm, c), lambda i: (i, 3)),
            pl.BlockSpec((None, CCONV_W, c), lambda i: (i_odd, 0, 0)),
            vec(), vec(), vec(), vec(), vec(),
            pl.BlockSpec((None, D_GROUPS, D_CHUNK, D_CHUNK), lambda i: (i_odd, 0, 0, 0)),
            pl.BlockSpec((None, D_CHUNK, D_GROUPS), lambda i: (i_odd, 0, 0)),
            pl.BlockSpec((None, CCONV_W - 1, c), lambda i: (i // tpb, 0, 0)),
        ],
        out_specs=[
            pl.BlockSpec((tm, 2 * c), lambda i: (i, 0)),
            pl.BlockSpec((None, CCONV_W - 1, c), lambda i: (i // tpb, 0, 0)),
            pl.BlockSpec((tm, c), lambda i: (i, 0)),
        ],
        out_shape=[jax.ShapeDtypeStruct((m, 2 * c), BF),
                   jax.ShapeDtypeStruct((nb, CCONV_W - 1, c), F32),
                   jax.ShapeDtypeStruct((m, c), F32)],
        scratch_shapes=[pltpu.VMEM((32 + tm, c), F32)],
        compiler_params=_params("arbitrary"),
        name="odd_post",
    )(z, z, z, z, cconv_w, r3(cconv_b), r3(c_ln_g), r3(c_ln_b), r3(d_ln_g), r3(d_ln_b), d_ws, d_bs_t, prev)


def _rope128(x, cos, sin_signed):
    lane = lax.broadcasted_iota(jnp.int32, x.shape, 1)
    swapped = jnp.where((lane & (HEAD_DIM - 1)) < HEAD_DIM // 2,
                        pltpu.roll(x, LANES - HEAD_DIM // 2, 1), pltpu.roll(x, HEAD_DIM // 2, 1))
    return x * cos + swapped * sin_signed


def _rope_slab(ref, c0, width, cos, sin_signed):
    return jnp.concatenate(
        [_rope128(ref[:, c0 + k * LANES:c0 + (k + 1) * LANES], cos, sin_signed) for k in range(width // LANES)],
        axis=1)


def _rope_prompt_kernel(zq_ref, zn_ref, zw_ref, cos_ref, sin_ref, q_ref, nsat_ref, wint_ref, kc_ref, vc_ref, *, ts):
    cos, sin = cos_ref[...], sin_ref[...]
    kvw = KV_HEADS * HEAD_DIM
    scale = HEAD_DIM ** -0.5
    for k in range(N_HEADS * HEAD_DIM // LANES):
        r = (_rope128(zq_ref[:, k * LANES:(k + 1) * LANES], cos, sin) * scale).astype(q_ref.dtype)
        q_ref[2 * k] = r[:, :HEAD_DIM]
        q_ref[2 * k + 1] = r[:, HEAD_DIM:]
    k_cmp = _rope_slab(zn_ref, 0, kvw, cos, sin)
    v_cmp = zn_ref[:, kvw:2 * kvw]
    k_sel = _rope_slab(zn_ref, 2 * kvw, kvw, cos, sin)
    v_sel = zn_ref[:, 3 * kvw:4 * kvw]
    k_win = _rope_slab(zw_ref, 0, kvw, cos, sin)
    v_win = zw_ref[:, kvw:2 * kvw]
    for t, slab in enumerate((k_cmp, v_cmp, k_sel, v_sel)):
        nsat_ref[t] = slab.T.reshape(KV_HEADS, HEAD_DIM, ts)
    for t, slab in enumerate((k_win, v_win)):
        wint_ref[t] = slab.T.reshape(KV_HEADS, HEAD_DIM, ts)
    nblk = ts // NSA_BLOCK
    kc = jnp.sum(k_cmp.reshape(nblk, NSA_BLOCK, kvw), axis=1) * (1.0 / NSA_BLOCK)
    vc = jnp.sum(v_cmp.reshape(nblk, NSA_BLOCK, kvw), axis=1) * (1.0 / NSA_BLOCK)
    for g in range(KV_HEADS):
        kc_ref[g] = kc[:, g * HEAD_DIM:(g + 1) * HEAD_DIM]
        vc_ref[g] = vc[:, g * HEAD_DIM:(g + 1) * HEAD_DIM]


def _rope_prompt(z, cos, sin, batch, seq, ts):
    tpb = seq // ts
    qw = N_HEADS * HEAD_DIM
    nblk = ts // NSA_BLOCK
    return pl.pallas_call(
        functools.partial(_rope_prompt_kernel, ts=ts),
        grid=(batch * tpb,),
        in_specs=[
            pl.BlockSpec((ts, qw), lambda i: (i, 3)),
            pl.BlockSpec((ts, qw), lambda i: (i, 4)),
            pl.BlockSpec((ts, qw // 2), lambda i: (i, 10)),
            pl.BlockSpec((ts, LANES), lambda i: (i % tpb, 0)),
            pl.BlockSpec((ts, LANES), lambda i: (i % tpb, 0)),
        ],
        out_specs=[
            pl.BlockSpec((None, N_HEADS, ts, HEAD_DIM), lambda i: (i // tpb, 0, i % tpb, 0)),
            pl.BlockSpec((None, 4, KV_HEADS, HEAD_DIM, ts), lambda i: (i // tpb, 0, 0, 0, i % tpb)),
            pl.BlockSpec((None, 2, KV_HEADS, HEAD_DIM, ts), lambda i: (i // tpb, 0, 0, 0, i % tpb)),
            pl.BlockSpec((None, KV_HEADS, nblk, HEAD_DIM), lambda i: (i // tpb, 0, i % tpb, 0)),
            pl.BlockSpec((None, KV_HEADS, nblk, HEAD_DIM), lambda i: (i // tpb, 0, i % tpb, 0)),
        ],
        out_shape=[
            jax.ShapeDtypeStruct((batch, N_HEADS, seq, HEAD_DIM), BF),
            jax.ShapeDtypeStruct((batch, 4, KV_HEADS, HEAD_DIM, seq), F32),
            jax.ShapeDtypeStruct((batch, 2, KV_HEADS, HEAD_DIM, seq), F32),
            jax.ShapeDtypeStruct((batch, KV_HEADS, seq // NSA_BLOCK, HEAD_DIM), F32),
            jax.ShapeDtypeStruct((batch, KV_HEADS, seq // NSA_BLOCK, HEAD_DIM), F32),
        ],
        compiler_params=_params("arbitrary"),
        name="rope_prompt",
    )(z, z, z, cos, sin)


def _rope_sample_kernel(zq_ref, zn_ref, zw_ref, cos_ref, sin_ref, q_ref, nsa_ref, win_ref):
    cos, sin = cos_ref[...], sin_ref[...]
    kvw = KV_HEADS * HEAD_DIM
    scale = HEAD_DIM ** -0.5
    for k in range(N_HEADS * HEAD_DIM // LANES):
        r = _rope128(zq_ref[:, k * LANES:(k + 1) * LANES], cos, sin) * scale
        q_ref[2 * k] = r[:, :HEAD_DIM]
        q_ref[2 * k + 1] = r[:, HEAD_DIM:]
    nsa_ref[:, 0:kvw] = _rope_slab(zn_ref, 0, kvw, cos, sin)
    nsa_ref[:, kvw:2 * kvw] = zn_ref[:, kvw:2 * kvw]
    nsa_ref[:, 2 * kvw:3 * kvw] = _rope_slab(zn_ref, 2 * kvw, kvw, cos, sin)
    nsa_ref[:, 3 * kvw:4 * kvw] = zn_ref[:, 3 * kvw:4 * kvw]
    win_ref[:, 0:kvw] = _rope_slab(zw_ref, 0, kvw, cos, sin)
    win_ref[:, kvw:2 * kvw] = zw_ref[:, kvw:2 * kvw]


def _rope_sample(z, cos, sin, batch, seq):
    qw = N_HEADS * HEAD_DIM
    return pl.pallas_call(
        _rope_sample_kernel,
        grid=(batch,),
        in_specs=[
            pl.BlockSpec((seq, qw), lambda i: (i, 3)),
            pl.BlockSpec((seq, qw), lambda i: (i, 4)),
            pl.BlockSpec((seq, qw // 2), lambda i: (i, 10)),
            pl.BlockSpec((seq, LANES), lambda i: (0, 0)),
            pl.BlockSpec((seq, LANES), lambda i: (0, 0)),
        ],
        out_specs=[
            pl.BlockSpec((None, N_HEADS, seq, HEAD_DIM), lambda i: (i, 0, 0, 0)),
            pl.BlockSpec((seq, qw), lambda i: (i, 0)),
            pl.BlockSpec((seq, qw // 2), lambda i: (i, 0)),
        ],
        out_shape=[
            jax.ShapeDtypeStruct((batch, N_HEADS, seq, HEAD_DIM), F32),
            jax.ShapeDtypeStruct((batch * seq, qw), F32),
            jax.ShapeDtypeStruct((batch * seq, qw // 2), F32),
        ],
        compiler_params=_params("arbitrary"),
        name="rope_sample",
    )(z, z, z, cos, sin)


def _cmp_branch(kc, vc, q, qpos, nq):
    nb = kc.shape[0]
    st = lax.dot_general(kc.astype(BF), q, (((1,), (1,)), ((), ())), preferred_element_type=F32)
    blk = lax.broadcasted_iota(jnp.int32, st.shape, 0)
    ok = (blk + 1) * NSA_BLOCK <= qpos + 1
    sm = jnp.where(ok, st, NEG)
    mx = jnp.max(sm, axis=0, keepdims=True)
    e = jnp.where(ok, jnp.exp(sm - mx), 0.0)
    den = jnp.sum(e, axis=0, keepdims=True)
    pt = e / jnp.where(den > 0.0, den, 1.0)
    o_cmp = lax.dot_general(pt.astype(BF), vc.astype(BF), (((0,), (0,)), ((), ())), preferred_element_type=F32)
    imp = pt[:, 0:nq]
    for r in range(1, GQA):
        imp = imp + pt[:, r * nq:(r + 1) * nq]
    return o_cmp, imp


def _importance(imp, qpos_q):
    blk = lax.broadcasted_iota(jnp.int32, imp.shape, 0)
    cur = qpos_q // NSA_BLOCK
    forced = (blk == 0) | (blk == cur) | (blk == cur - 1)
    imp = jnp.where(forced, GQA + 1.0, imp)
    return jnp.where(blk <= cur, imp, -1.0)


def _select_topk(imp_ref, nb):
    imp = imp_ref[...]
    blk = lax.broadcasted_iota(jnp.int32, imp.shape, 0)

    def body(i, rank):
        row = imp_ref[pl.ds(i, 1), :]
        ahead = (row > imp) | ((row == imp) & (i < blk))
        return rank + jnp.where(ahead, 1.0, 0.0)

    rank = lax.fori_loop(0, nb, body, jnp.zeros(imp.shape, F32))
    return jnp.where(rank < float(N_SEL), 1.0, 0.0)


def _expand_blocks(sel_t, n_keys, first_block):
    nbl = sel_t.shape[0]
    kb = lax.broadcasted_iota(jnp.int32, (nbl, n_keys), 1) // NSA_BLOCK + first_block
    nn = lax.broadcasted_iota(jnp.int32, (nbl, n_keys), 0)
    e = jnp.where(kb == nn, 1.0, 0.0).astype(BF)
    return lax.dot_general(sel_t.astype(BF), e, (((0,), (0,)), ((), ())), preferred_element_type=F32)


def _online_update(s, valid, vt, m_ref, l_ref, acc_ref, idx, v_rows=None):
    nk = s.shape[1]
    s = jnp.where(valid, s, NEG)
    m_prev = m_ref[idx]
    m_new = jnp.maximum(m_prev, jnp.max(s, axis=1, keepdims=True))
    alpha = jnp.exp(m_prev - m_new)
    p = jnp.where(valid, jnp.exp(s - jnp.concatenate([m_new] * (nk // LANES), axis=1)), 0.0)
    l_ref[idx] = alpha * l_ref[idx] + jnp.sum(p, axis=1, keepdims=True)
    if v_rows is None:
        pv = lax.dot_general(p.astype(BF), vt, (((1,), (1,)), ((), ())), preferred_element_type=F32)
    else:
        pv = jnp.dot(p.astype(BF), v_rows, preferred_element_type=F32)
    acc_ref[idx] = alpha[:, :HEAD_DIM] * acc_ref[idx] + pv
    m_ref[idx] = m_new


def _nsa_prompt_kernel(q_ref, kc_ref, vc_ref, ks_ref, vs_ref, kw_ref, vw_ref, gl_ref, o_ref,
                       imp_ref, m_ref, l_ref, acc_ref, *, tq, seq):
    qi = pl.program_id(2)
    q0 = qi * tq
    rq = GQA * tq
    nb = seq // NSA_BLOCK
    q = q_ref[...].reshape(rq, HEAD_DIM)

    col = lax.broadcasted_iota(jnp.int32, (nb, rq), 1)
    o_cmp, imp = _cmp_branch(kc_ref[...], vc_ref[...], q, q0 + (col & (tq - 1)), tq)
    qpos_q = q0 + lax.broadcasted_iota(jnp.int32, (nb, tq), 1)
    imp_ref[...] = _importance(imp, qpos_q)
    sel_t = _select_topk(imp_ref, nb)

    m_ref[...] = jnp.full(m_ref.shape, NEG, F32)
    l_ref[...] = jnp.zeros(l_ref.shape, F32)
    acc_ref[...] = jnp.zeros(acc_ref.shape, F32)

    qpos = q0 + lax.broadcasted_iota(jnp.int32, (tq, tq), 0)
    win_chunks = -(-(WINDOW - 1) // tq)
    for c in range(seq // tq):
        @pl.when(c <= qi)
        def _(c=c):
            kpos = c * tq + lax.broadcasted_iota(jnp.int32, (tq, tq), 1)
            causal = kpos <= qpos
            selx = _expand_blocks(sel_t, tq, c * (tq // NSA_BLOCK))
            valid = jnp.concatenate([jnp.where(causal, selx, 0.0)] * GQA, axis=0) > 0.5
            s = jnp.dot(q, ks_ref[:, c * tq:(c + 1) * tq].astype(BF), preferred_element_type=F32)
            _online_update(s, valid, vs_ref[:, c * tq:(c + 1) * tq].astype(BF), m_ref, l_ref, acc_ref, 0)

            @pl.when(c >= qi - win_chunks)
            def _():
                inwin = jnp.where(causal & (kpos > qpos - WINDOW), 1.0, 0.0)
                validw = jnp.concatenate([inwin] * GQA, axis=0) > 0.5
                sw = jnp.dot(q, kw_ref[:, c * tq:(c + 1) * tq].astype(BF), preferred_element_type=F32)
                _online_update(sw, validw, vw_ref[:, c * tq:(c + 1) * tq].astype(BF), m_ref, l_ref, acc_ref, 1)

    o_sel = acc_ref[0] / l_ref[0][:, :HEAD_DIM]
    o_win = acc_ref[1] / l_ref[1][:, :HEAD_DIM]
    gate = _sigmoid(gl_ref[...])
    outs = []
    for r in range(GQA):
        rows = slice(r * tq, (r + 1) * tq)
        outs.append(gate[:, 3 * r:3 * r + 1] * o_cmp[rows] + gate[:, 3 * r + 1:3 * r + 2] * o_sel[rows]
                    + gate[:, 3 * r + 2:3 * r + 3] * o_win[rows])
    o_ref[...] = jnp.concatenate(outs, axis=1).astype(o_ref.dtype)


def _nsa_prompt(q, kc, vc, nsat, wint, gl, batch, seq, tq):
    nq = seq // tq
    nb = seq // NSA_BLOCK
    rq = GQA * tq
    kv_spec = lambda t: pl.BlockSpec((None, None, None, HEAD_DIM, seq), lambda b, g, i: (b, t, g, 0, 0))
    cmp_spec = pl.BlockSpec((None, None, nb, HEAD_DIM), lambda b, g, i: (b, g, 0, 0))
    return pl.pallas_call(
        functools.partial(_nsa_prompt_kernel, tq=tq, seq=seq),
        grid=(batch, KV_HEADS, nq),
        in_specs=[
            pl.BlockSpec((None, GQA, tq, HEAD_DIM), lambda b, g, i: (b, g, i, 0)),
            cmp_spec, cmp_spec,
            kv_spec(2), kv_spec(3), kv_spec(0), kv_spec(1),
            pl.BlockSpec((tq, LANES), lambda b, g, i: (b * nq + i, g)),
        ],
        out_specs=pl.BlockSpec((tq, GQA * HEAD_DIM), lambda b, g, i: (b * nq + i, g)),
        out_shape=jax.ShapeDtypeStruct((batch * seq, N_HEADS * HEAD_DIM), BF),
        scratch_shapes=[
            pltpu.VMEM((nb, tq), F32),
            pltpu.VMEM((2, rq, LANES), F32),
            pltpu.VMEM((2, rq, LANES), F32),
            pltpu.VMEM((2, rq, HEAD_DIM), F32),
        ],
        compiler_params=_params("arbitrary", "arbitrary", "arbitrary"),
        name="nsa_prompt",
    )(q, kc, vc, nsat, nsat, wint, wint, gl)


def _cmp_means_kernel(pt_ref, *refs, n_pages):
    page_refs = refs[:n_pages]
    kc_ref, vc_ref = refs[n_pages], refs[n_pages + 1]
    bpp = PAGE_SIZE // NSA_BLOCK
    kvw = KV_HEADS * HEAD_DIM
    for t, out in enumerate((kc_ref, vc_ref)):
        x = jnp.concatenate([page_refs[p][t].reshape(kvw, PAGE_SIZE).T for p in range(n_pages)], axis=0)
        out[...] = jnp.sum(x.reshape(n_pages * bpp, NSA_BLOCK, kvw), axis=1) * (1.0 / NSA_BLOCK)


def _cmp_means(cache_t, layer, page_table, n_pages):
    batch, ppb = page_table.shape
    kvw = KV_HEADS * HEAD_DIM
    bpp = PAGE_SIZE // NSA_BLOCK
    steps = ppb // n_pages

    def page_spec(p):
        return pl.BlockSpec((None, None, 2, KV_HEADS, HEAD_DIM, PAGE_SIZE),
                            lambda b, s, pt: (layer, pt[b, s * n_pages + p], 0, 0, 0, 0))

    out_spec = pl.BlockSpec((None, n_pages * bpp, kvw), lambda b, s, pt: (b, s, 0))
    return pl.pallas_call(
        functools.partial(_cmp_means_kernel, n_pages=n_pages),
        grid_spec=pltpu.PrefetchScalarGridSpec(
            num_scalar_prefetch=1, grid=(batch, steps),
            in_specs=[page_spec(p) for p in range(n_pages)],
            out_specs=[out_spec, out_spec]),
        out_shape=[jax.ShapeDtypeStruct((batch, ppb * bpp, kvw), F32)] * 2,
        compiler_params=_params("arbitrary", "arbitrary"),
        name="cmp_means",
    )(page_table, *([cache_t] * n_pages))


def _nsa_sample_kernel(pt_ref, *refs, n_pages, past, s_new):
    (q_ref, kc_ref, vc_ref, new_ref, wnew_ref, wbuf_ref, gl_ref) = refs[:7]
    page_refs = refs[7:7 + n_pages]
    o_ref = refs[7 + n_pages]
    kcf, vcf, imp_ref, sel_ref, m_ref, l_ref, acc_ref, ocmp_ref = refs[8 + n_pages:]
    step = pl.program_id(1)
    nsteps = pl.num_programs(1)
    kvw = KV_HEADS * HEAD_DIM
    rq = GQA * s_new
    nbp = past // NSA_BLOCK
    nbf = kcf.shape[0]
    wlen = wbuf_ref.shape[-1]
    pad_rows = LANES - s_new

    def q_of(g):
        return q_ref[g * GQA:(g + 1) * GQA].reshape(rq, HEAD_DIM).astype(BF)

    def pad_keys(x):
        return jnp.concatenate([x, jnp.zeros((pad_rows, HEAD_DIM), F32)], axis=0).astype(BF)

    @pl.when(step == 0)
    def _():
        row8 = lax.broadcasted_iota(jnp.int32, (nbf - nbp, kvw), 0)
        for full, src, c0 in ((kcf, kc_ref, 0), (vcf, vc_ref, kvw)):
            full[0:nbp, :] = src[...]
            mean_new = jnp.sum(new_ref[:, c0:c0 + kvw], axis=0, keepdims=True) * (1.0 / NSA_BLOCK)
            full[nbp:nbf, :] = jnp.where(row8 == 0, mean_new, 0.0)
        col = lax.broadcasted_iota(jnp.int32, (nbf, rq), 1)
        qpos_q = past + lax.broadcasted_iota(jnp.int32, (nbf, s_new), 1)
        for g in range(KV_HEADS):
            lanes = slice(g * HEAD_DIM, (g + 1) * HEAD_DIM)
            o_cmp, imp = _cmp_branch(kcf[:, lanes], vcf[:, lanes], q_of(g), past + (col & (s_new - 1)), s_new)
            ocmp_ref[g] = o_cmp
            imp_ref[:, g * s_new:(g + 1) * s_new] = _importance(imp, qpos_q)
        sel = _select_topk(imp_ref, nbf)
        for g in range(KV_HEADS):
            sel_ref[g] = jnp.concatenate([sel[:, g * s_new:(g + 1) * s_new]] * GQA, axis=1)
        m_ref[...] = jnp.full(m_ref.shape, NEG, F32)
        l_ref[...] = jnp.zeros(l_ref.shape, F32)
        acc_ref[...] = jnp.zeros(acc_ref.shape, F32)

    nk = n_pages * PAGE_SIZE
    nbl = nk // NSA_BLOCK
    for g in range(KV_HEADS):
        kt = jnp.concatenate([page_refs[p][0, g] for p in range(n_pages)], axis=1).astype(BF)
        vt = jnp.concatenate([page_refs[p][1, g] for p in range(n_pages)], axis=1).astype(BF)
        sel_rows = sel_ref[g, pl.ds(pl.multiple_of(step * nbl, nbl), nbl), :]
        valid = _expand_blocks(sel_rows, nk, 0) > 0.5
        s = jnp.dot(q_of(g), kt, preferred_element_type=F32)
        _online_update(s, valid, vt, m_ref, l_ref, acc_ref, g)

    @pl.when(step == nsteps - 1)
    def _():
        tq_col = lax.broadcasted_iota(jnp.int32, (rq, LANES), 0) & (s_new - 1)
        tk = lax.broadcasted_iota(jnp.int32, (rq, LANES), 1)
        new_ok = (tk <= tq_col) & (tk < s_new)
        gate = _sigmoid(gl_ref[...])
        tail = nbf - 16
        for g in range(KV_HEADS):
            q = q_of(g)
            lane0 = 2 * kvw + g * HEAD_DIM
            k_new = pad_keys(new_ref[:, lane0:lane0 + HEAD_DIM])
            v_new = pad_keys(new_ref[:, lane0 + kvw:lane0 + kvw + HEAD_DIM])
            kb = lax.broadcasted_iota(jnp.int32, (16, LANES), 0)
            e = jnp.where(kb == nbp - tail, 1.0, 0.0).astype(BF)
            selx = lax.dot_general(sel_ref[g, tail:nbf, :].astype(BF), e, (((0,), (0,)), ((), ())),
                                   preferred_element_type=F32)
            s = lax.dot_general(q, k_new, (((1,), (1,)), ((), ())), preferred_element_type=F32)
            _online_update(s, (selx > 0.5) & new_ok, None, m_ref, l_ref, acc_ref, g, v_rows=v_new)
            o_sel = acc_ref[g] / l_ref[g][:, :HEAD_DIM]
            wl0 = g * HEAD_DIM
            kw_new = pad_keys(wnew_ref[:, wl0:wl0 + HEAD_DIM])
            vw_new = pad_keys(wnew_ref[:, kvw + wl0:kvw + wl0 + HEAD_DIM])
            sb = jnp.dot(q, wbuf_ref[0, g].astype(BF), preferred_element_type=F32)
            sn = lax.dot_general(q, kw_new, (((1,), (1,)), ((), ())), preferred_element_type=F32)
            jb = lax.broadcasted_iota(jnp.int32, (rq, wlen), 1)
            tq_b = lax.broadcasted_iota(jnp.int32, (rq, wlen), 0) & (s_new - 1)
            ok_b = (past - wlen + jb > past + tq_b - WINDOW) & (past - wlen + jb >= 0)
            sb = jnp.where(ok_b, sb, NEG)
            sn = jnp.where(new_ok, sn, NEG)
            mx = jnp.maximum(jnp.max(sb, axis=1, keepdims=True), jnp.max(sn, axis=1, keepdims=True))
            pb = jnp.where(ok_b, jnp.exp(sb - mx), 0.0)
            pn = jnp.where(new_ok, jnp.exp(sn - mx), 0.0)
            den = jnp.sum(pb, axis=1, keepdims=True) + jnp.sum(pn, axis=1, keepdims=True)
            o_win = (lax.dot_general(pb.astype(BF), wbuf_ref[1, g].astype(BF), (((1,), (1,)), ((), ())),
                                     preferred_element_type=F32)
                     + jnp.dot(pn.astype(BF), vw_new, preferred_element_type=F32)) / den
            o_cmp = ocmp_ref[g]
            for r in range(GQA):
                rows = slice(r * s_new, (r + 1) * s_new)
                c = g * LANES + 3 * r
                h = g * GQA + r
                o_ref[:, h * HEAD_DIM:(h + 1) * HEAD_DIM] = (
                    gate[:, c:c + 1] * o_cmp[rows] + gate[:, c + 1:c + 2] * o_sel[rows]
                    + gate[:, c + 2:c + 3] * o_win[rows]).astype(o_ref.dtype)


def _nsa_sample(cache_t, layer, page_table, q, kc, vc, nsa_new, win_new, wbuf_t, gl, n_pages, past, s_new):
    batch, ppb = page_table.shape
    kvw = KV_HEADS * HEAD_DIM
    qw = N_HEADS * HEAD_DIM
    nbp = past // NSA_BLOCK
    nbf = nbp + 8
    wlen = wbuf_t.shape[-1]
    rq = GQA * s_new
    steps = ppb // n_pages

    def page_spec(p):
        return pl.BlockSpec((None, None, 2, KV_HEADS, HEAD_DIM, PAGE_SIZE),
                            lambda b, s, pt: (layer, pt[b, s * n_pages + p], 1, 0, 0, 0))

    per_b = lambda shape: pl.BlockSpec((None,) + shape, lambda b, s, pt: (b,) + (0,) * len(shape))
    rows_b = lambda w: pl.BlockSpec((s_new, w), lambda b, s, pt: (b, 0))
    return pl.pallas_call(
        functools.partial(_nsa_sample_kernel, n_pages=n_pages, past=past, s_new=s_new),
        grid_spec=pltpu.PrefetchScalarGridSpec(
            num_scalar_prefetch=1, grid=(batch, steps),
            in_specs=[
                per_b((N_HEADS, s_new, HEAD_DIM)),
                per_b((nbp, kvw)), per_b((nbp, kvw)),
                rows_b(4 * kvw), rows_b(2 * kvw),
                pl.BlockSpec((None, None, 2, KV_HEADS, HEAD_DIM, wlen), lambda b, s, pt: (layer, b, 0, 0, 0, 0)),
                rows_b(KV_HEADS * LANES),
            ] + [page_spec(p) for p in range(n_pages)],
            out_specs=rows_b(qw),
            scratch_shapes=[
                pltpu.VMEM((nbf, kvw), F32), pltpu.VMEM((nbf, kvw), F32),
                pltpu.VMEM((nbf, KV_HEADS * s_new), F32),
                pltpu.VMEM((KV_HEADS, nbf, rq), F32),
                pltpu.VMEM((KV_HEADS, rq, LANES), F32),
                pltpu.VMEM((KV_HEADS, rq, LANES), F32),
                pltpu.VMEM((KV_HEADS, rq, HEAD_DIM), F32),
                pltpu.VMEM((KV_HEADS, rq, HEAD_DIM), F32),
            ]),
        out_shape=jax.ShapeDtypeStruct((batch * s_new, qw), BF),
        compiler_params=_params("arbitrary", "arbitrary"),
        name="nsa_sample",
    )(page_table, q, kc, vc, nsa_new, win_new, wbuf_t, gl, *([cache_t] * n_pages))


def _rope_tables(pos):
    half = HEAD_DIM // 2
    freq = ROPE_THETA ** (-jnp.arange(half, dtype=F32) / half)
    ang = pos.astype(F32)[:, None] * freq[None, :]
    cos, sin = jnp.cos(ang), jnp.sin(ang)
    return jnp.tile(cos, (1, LANES // half)), jnp.tile(jnp.concatenate([-sin, sin], axis=1), (1, LANES // HEAD_DIM))


def _trunk(x, mod5, b_off, batch, seq, pos, sconv_prev, cconv_prev, ffn_prev, nsa_fn, tm, mm_tm, gate_of, wts):
    (g_mix, g_ffn, g_final, w_in_even_t, w_gate_t, sconv_w, w_out_even, w_in_odd, cconv_w, cconv_b, c_ln_g,
     c_ln_b, d_ln_g, d_ln_b, d_ws, d_bs_t, w_out_odd, w_up, ffn_conv_w, w_down) = wts
    depth = g_mix.shape[0]
    d = x.shape[1]
    tpb = seq // tm
    a_w = sconv_w.shape[-1]
    dff = ffn_conv_w.shape[-1]
    qw = N_HEADS * HEAD_DIM
    kv_cols = 3 * a_w + qw + 6 * KV_HEADS * HEAD_DIM
    cos, sin = _rope_tables(pos)
    new_nsa, new_win, new_s, new_c, new_dv, new_f = [], [], [], [], [], []
    for l in range(depth):
        i = l // 2
        h = _norm_mod(x, g_mix, mod5, l, 0, b_off, tm, tpb)
        gate_spec, gate = gate_of(l, 2, mm_tm)
        if l % 2 == 0:
            z = _mm([h], w_in_even_t, i, kv_cols, mm_tm, 512, wt=True, name="in_even")
            gl = _mm([h], w_gate_t, i, KV_HEADS * LANES, mm_tm, 512, wt=True, name="gate_logits")
            mix_a, sb = _mixer_a(z, sconv_w, i, sconv_prev[i], tm, tpb)
            o_b, nsa_rows, win_state = nsa_fn(i, z, gl, cos, sin)
            x = _mm([mix_a, o_b], w_out_even, i, d, mm_tm, 512, res=x, gate_spec=gate_spec, gate=gate,
                    name="out_even")
            new_s.append(sb)
            new_nsa.append(nsa_rows)
            new_win.append(win_state)
        else:
            z = _mm([h], w_in_odd, i, w_in_odd.shape[-1], mm_tm, 512, name="in_odd")
            mix, cb, v = _odd_post(z, i, cconv_prev[i], cconv_w, cconv_b, c_ln_g, c_ln_b, d_ln_g, d_ln_b,
                                   d_ws, d_bs_t, tm, tpb)
            x = _mm([mix], w_out_odd, i, d, mm_tm, 512, res=x, gate_spec=gate_spec, gate=gate, name="out_odd")
            new_c.append(cb)
            new_dv.append(v)
        h = _norm_mod(x, g_ffn, mod5, l, 3, b_off, tm, tpb)
        zu = _mm([h], w_up, l, 2 * dff, mm_tm, 512, name="ffn_up")
        act, fb = _ffn_act(zu, ffn_conv_w, l, ffn_prev[l], tm, tpb, 512)
        down_tm = min(mm_tm, 512)
        gate_spec, gate = gate_of(l, 5, down_tm)
        x = _mm([act], w_down, l, d, down_tm, 512, res=x, gate_spec=gate_spec, gate=gate, name="ffn_down")
        new_f.append(fb)
    y = _final_norm(x, g_final, tm)
    return y, new_nsa, new_win, new_s, new_c, new_dv, new_f


def kernel(x_prompt, x_sample, cache_nsa_kv, state_win_kv, state_sconv, state_cconv, state_ffn_conv, page_table, c_prompt, c_sample, g_mix, g_ffn, g_final, w_ada, b_ada, w_in_even, sconv_w, w_out_even, w_in_odd, cconv_w, cconv_b, c_ln_g, c_ln_b, d_ln_g, d_ln_b, d_ws, d_bs, w_out_odd, w_up, ffn_conv_w, w_down):
    bp, sp, d = x_prompt.shape
    bs, ss, _ = x_sample.shape
    depth = g_mix.shape[0]
    n_even = w_in_even.shape[0]
    past = page_table.shape[1] * PAGE_SIZE
    a_w = sconv_w.shape[-1]
    dff = ffn_conv_w.shape[-1]
    kvw = KV_HEADS * HEAD_DIM
    dt = x_prompt.dtype

    rows = -(-(bp + bs) // 8) * 8
    c_all = jnp.concatenate([c_prompt, c_sample, jnp.zeros((rows - bp - bs, d), dt)], axis=0)
    mod4 = _ada(c_all, w_ada, b_ada)
    mod5 = mod4.reshape(depth, 6, rows, 1, d)

    gate_c0 = 3 * a_w + N_HEADS * HEAD_DIM + 6 * kvw
    w_in_even_t = jnp.swapaxes(w_in_even, 1, 2)
    wg = w_in_even_t[:, gate_c0:, :].reshape(n_even, KV_HEADS, GQA * 3, d)
    w_gate_t = jnp.pad(wg, ((0, 0), (0, 0), (0, LANES - GQA * 3), (0, 0))).reshape(n_even, KV_HEADS * LANES, d)

    wts = (g_mix, g_ffn, g_final, w_in_even_t, w_gate_t, sconv_w, w_out_even, w_in_odd, cconv_w, cconv_b, c_ln_g,
           c_ln_b, d_ln_g, d_ln_b, d_ws, jnp.swapaxes(d_bs, 1, 2), w_out_odd, w_up, ffn_conv_w, w_down)

    tm_p = 512
    tq = 256

    def prompt_nsa(i, z, gl, cos, sin):
        q, nsat, wint, kc, vc = _rope_prompt(z, cos, sin, bp, sp, tm_p)
        o_b = _nsa_prompt(q, kc, vc, nsat, wint, gl, bp, sp, tq)
        keep = min(WINDOW, sp)
        nsa_rows = jnp.transpose(nsat, (0, 4, 1, 2, 3))
        win_state = jnp.transpose(wint[..., sp - keep:], (0, 4, 1, 2, 3))
        return o_b, nsa_rows, win_state

    def prompt_gate(l, which, tm):
        return (pl.BlockSpec((None, None, None, 1, 512), lambda j, i: (l, which, (i * tm) // sp, 0, j)), mod5)

    mm_tm_p = 1024
    zeros = lambda n, r, w: jnp.zeros((n, bp, r, w), dt)
    (y_p, nsa_p, win_p, s_p, c_p, _, f_p) = _trunk(
        x_prompt.reshape(bp * sp, d), mod5, 0, bp, sp, jnp.arange(sp, dtype=jnp.int32),
        zeros(n_even, 2, a_w), zeros(depth // 2, CCONV_W - 1, a_w), zeros(depth, 2, dff),
        prompt_nsa, tm_p, mm_tm_p, prompt_gate, wts)

    cache_t = jnp.transpose(cache_nsa_kv, (0, 1, 3, 4, 5, 2))
    wbuf_t = jnp.transpose(state_win_kv, (0, 1, 3, 4, 5, 2))
    n_pages = 8

    def sample_nsa(i, z, gl, cos, sin):
        q, nsa_new, win_new = _rope_sample(z, cos, sin, bs, ss)
        kc, vc = _cmp_means(cache_t, i, page_table, n_pages)
        o_b = _nsa_sample(cache_t, i, page_table, q, kc, vc, nsa_new, win_new, wbuf_t, gl, n_pages, past, ss)
        win_new_t = jnp.transpose(win_new.reshape(bs, ss, 2, KV_HEADS, HEAD_DIM), (0, 2, 3, 4, 1))
        win_t = jnp.concatenate([wbuf_t[i], win_new_t], axis=-1)[..., ss:]
        win_state = jnp.transpose(win_t, (0, 4, 1, 2, 3))
        return o_b, nsa_new.reshape(bs, ss, 4, KV_HEADS, HEAD_DIM), win_state

    def sample_gate(l, which, tm):
        gate = jnp.repeat(mod4[l, which, bp:bp + bs], ss, axis=0)
        return (pl.BlockSpec((bs * ss, 512), lambda j, i: (0, j)), gate)

    (y_s, nsa_s, win_s, s_s, c_s, dv_s, f_s) = _trunk(
        x_sample.reshape(bs * ss, d), mod5, bp, bs, ss, past + jnp.arange(ss, dtype=jnp.int32),
        state_sconv, state_cconv, state_ffn_conv, sample_nsa, ss, bs * ss, sample_gate, wts)

    nsa_p = [a.reshape(bp, sp, 4, KV_HEADS, HEAD_DIM) for a in nsa_p]
    win_p = [a.reshape(bp, -1, 2, KV_HEADS, HEAD_DIM) for a in win_p]
    win_s = [a.reshape(bs, -1, 2, KV_HEADS, HEAD_DIM) for a in win_s]
    dv_s = [a.reshape(bs, ss, -1) for a in dv_s]
    return (y_p.reshape(bp, sp, d), y_s.reshape(bs, ss, d), jnp.stack(nsa_p), jnp.stack(nsa_s),
            jnp.stack(win_p), jnp.stack(win_s), jnp.stack(s_p), jnp.stack(s_s), jnp.stack(c_p),
            jnp.stack(c_s), jnp.stack(dv_s), jnp.stack(f_p), jnp.stack(f_s))
```

```python
import functools
import math

import jax
import jax.numpy as jnp
from jax import lax
from jax.experimental import pallas as pl
from jax.experimental.pallas import tpu as pltpu

BF = jnp.bfloat16
F32 = jnp.float32

HEAD_DIM = 64
N_HEADS = 16
KV_HEADS = 4
GQA = N_HEADS // KV_HEADS
NSA_BLOCK = 64
N_SEL = 16
WINDOW = 512
PAGE_SIZE = 128
ROPE_THETA = 10000.0
CCONV_W = 31
D_CHUNK = 128
D_GROUPS = 4
EPS = 1e-6
NEG = -1e30

LANES = 128
VMEM_LIMIT = 56 * 1024 * 1024


def _params(*sem):
    return pltpu.CompilerParams(dimension_semantics=sem, vmem_limit_bytes=VMEM_LIMIT)


def _sigmoid(x):
    return 1.0 / (1.0 + jnp.exp(-x))


def _silu(x):
    return x * _sigmoid(x)


def _gelu_tanh(x):
    return 0.5 * x * (1.0 + jnp.tanh(math.sqrt(2.0 / math.pi) * (x + 0.044715 * (x * x * x))))


def _layernorm(x, g, b):
    mu = jnp.mean(x, axis=-1, keepdims=True)
    xc = x - mu
    return xc * lax.rsqrt(jnp.mean(xc * xc, axis=-1, keepdims=True) + EPS) * g + b


def _ada_kernel(c_ref, w_ref, b_ref, o_ref):
    ca = _silu(c_ref[...]).astype(BF)
    acc = jnp.dot(ca, w_ref[...].astype(BF), preferred_element_type=F32)
    o_ref[...] = acc + b_ref[...]


def _ada(c16, w_ada, b_ada):
    depth, d, n6 = w_ada.shape
    rows = c16.shape[0]
    tn = 1024
    per = d // tn
    return pl.pallas_call(
        _ada_kernel,
        grid=(depth, n6 // tn),
        in_specs=[
            pl.BlockSpec((rows, d), lambda l, j: (0, 0)),
            pl.BlockSpec((None, d, tn), lambda l, j: (l, 0, j)),
            pl.BlockSpec((None, 1, tn), lambda l, j: (l, 0, j)),
        ],
        out_specs=pl.BlockSpec((None, None, rows, tn), lambda l, j: (l, j // per, 0, j % per)),
        out_shape=jax.ShapeDtypeStruct((depth, 6, rows, d), F32),
        compiler_params=_params("arbitrary", "arbitrary"),
        name="ada",
    )(c16, w_ada, b_ada.reshape(depth, 1, n6))


def _norm_mod_kernel(x_ref, g_ref, sh_ref, sc_ref, o_ref):
    x = x_ref[...]
    y = x * lax.rsqrt(jnp.mean(x * x, axis=-1, keepdims=True) + EPS) * g_ref[...]
    o_ref[...] = (y * (1.0 + sc_ref[...]) + sh_ref[...]).astype(o_ref.dtype)


def _norm_mod(x, g, mod5, l, which, b_off, tm, tpb):
    m, d = x.shape
    mod_spec = lambda w: pl.BlockSpec((None, None, None, 1, d),
                                      lambda i: (l, w, b_off + i // tpb, 0, 0))
    return pl.pallas_call(
        _norm_mod_kernel,
        grid=(m // tm,),
        in_specs=[
            pl.BlockSpec((tm, d), lambda i: (i, 0)),
            pl.BlockSpec((None, 1, d), lambda i: (l, 0, 0)),
            mod_spec(which), mod_spec(which + 1),
        ],
        out_specs=pl.BlockSpec((tm, d), lambda i: (i, 0)),
        out_shape=jax.ShapeDtypeStruct((m, d), BF),
        compiler_params=_params("arbitrary"),
        name="norm_mod",
    )(x, g.reshape(g.shape[0], 1, d), mod5, mod5)


def _final_norm_kernel(x_ref, g_ref, o_ref):
    x = x_ref[...]
    o_ref[...] = x * lax.rsqrt(jnp.mean(x * x, axis=-1, keepdims=True) + EPS) * g_ref[...]


def _final_norm(x, g, tm):
    m, d = x.shape
    return pl.pallas_call(
        _final_norm_kernel,
        grid=(m // tm,),
        in_specs=[pl.BlockSpec((tm, d), lambda i: (i, 0)), pl.BlockSpec((1, d), lambda i: (0, 0))],
        out_specs=pl.BlockSpec((tm, d), lambda i: (i, 0)),
        out_shape=jax.ShapeDtypeStruct((m, d), F32),
        compiler_params=_params("arbitrary"),
        name="final_norm",
    )(x, g.reshape(1, d))


def _mm_kernel(*refs, k_sizes, res_gate, wt):
    n_a = len(k_sizes)
    a_refs = refs[:n_a]
    w_ref = refs[n_a]
    pos = n_a + 1
    if res_gate:
        res_ref, gate_ref = refs[pos], refs[pos + 1]
        pos += 2
    o_ref, wb_ref = refs[pos], refs[pos + 1]

    @pl.when(pl.program_id(1) == 0)
    def _():
        wb_ref[...] = w_ref[...].astype(BF)

    acc = None
    k0 = 0
    for a_ref, ks in zip(a_refs, k_sizes):
        if wt:
            part = lax.dot_general(a_ref[...], wb_ref[:, k0:k0 + ks], (((1,), (1,)), ((), ())),
                                   preferred_element_type=F32)
        else:
            part = jnp.dot(a_ref[...], wb_ref[k0:k0 + ks, :], preferred_element_type=F32)
        acc = part if acc is None else acc + part
        k0 += ks
    if res_gate:
        acc = res_ref[...] + gate_ref[...] * acc
    o_ref[...] = acc.astype(o_ref.dtype)


def _mm(a_list, w, l, n, tm, tn, wt=False, res=None, gate_spec=None, gate=None, name="mm"):
    m = a_list[0].shape[0]
    k_sizes = tuple(a.shape[1] for a in a_list)
    k = sum(k_sizes)
    assert w.shape[2 if wt else 1] == k and n % tn == 0 and m % tm == 0
    in_specs = [pl.BlockSpec((tm, ks), lambda j, i: (i, 0)) for ks in k_sizes]
    if wt:
        in_specs.append(pl.BlockSpec((None, tn, k), lambda j, i: (l, j, 0)))
    else:
        in_specs.append(pl.BlockSpec((None, k, tn), lambda j, i: (l, 0, j)))
    args = list(a_list) + [w]
    if res is not None:
        in_specs += [pl.BlockSpec((tm, tn), lambda j, i: (i, j)), gate_spec]
        args += [res, gate]
    return pl.pallas_call(
        functools.partial(_mm_kernel, k_sizes=k_sizes, res_gate=res is not None, wt=wt),
        grid=(n // tn, m // tm),
        in_specs=in_specs,
        out_specs=pl.BlockSpec((tm, tn), lambda j, i: (i, j)),
        out_shape=jax.ShapeDtypeStruct((m, n), F32),
        scratch_shapes=[pltpu.VMEM((tn, k) if wt else (k, tn), BF)],
        compiler_params=_params("arbitrary", "arbitrary"),
        name=name,
    )(*args)


def _conv3(u, p, w):
    row = lax.broadcasted_iota(jnp.int32, u.shape, 0)
    um1 = jnp.where(row == 0, p[1:2], pltpu.roll(u, 1, 0))
    um2 = jnp.where(row == 0, p[0:1], jnp.where(row == 1, p[1:2], pltpu.roll(u, 2, 0)))
    return w[0:1] * um2 + w[1:2] * um1 + w[2:3] * u


def _prev_rows(i, tpb, prev_ref, carry_ref):
    if tpb == 1:
        return prev_ref[...]
    return jnp.where(i % tpb == 0, prev_ref[...], carry_ref[6:8, :])


def _mixer_a_kernel(ain_ref, ab_ref, ac_ref, w_ref, prev_ref, o_ref, st_ref, carry_ref, *, tm, tpb):
    i = pl.program_id(0)
    u = ac_ref[...] * ain_ref[...]
    p = _prev_rows(i, tpb, prev_ref, carry_ref)
    o_ref[...] = (ab_ref[...] * _conv3(u, p, w_ref[...])).astype(o_ref.dtype)
    st_ref[...] = u[tm - 2:tm]
    if tpb > 1:
        carry_ref[...] = u[tm - 8:tm]


def _mixer_a(z, sconv_w, l, prev, tm, tpb):
    m = z.shape[0]
    c = prev.shape[-1]
    nb = prev.shape[0]
    return pl.pallas_call(
        functools.partial(_mixer_a_kernel, tm=tm, tpb=tpb),
        grid=(m // tm,),
        in_specs=[
            pl.BlockSpec((tm, c), lambda i: (i, 0)),
            pl.BlockSpec((tm, c), lambda i: (i, 1)),
            pl.BlockSpec((tm, c), lambda i: (i, 2)),
            pl.BlockSpec((None, 3, c), lambda i: (l, 0, 0)),
            pl.BlockSpec((None, 2, c), lambda i: (i // tpb, 0, 0)),
        ],
        out_specs=[
            pl.BlockSpec((tm, c), lambda i: (i, 0)),
            pl.BlockSpec((None, 2, c), lambda i: (i // tpb, 0, 0)),
        ],
        out_shape=[jax.ShapeDtypeStruct((m, c), BF), jax.ShapeDtypeStruct((nb, 2, c), F32)],
        scratch_shapes=[pltpu.VMEM((8, c), F32)],
        compiler_params=_params("arbitrary"),
        name="mixer_a",
    )(z, z, z, sconv_w, prev)


def _ffn_act_kernel(a_ref, g_ref, w_ref, prev_ref, o_ref, st_ref, carry_ref, *, tm, tpb):
    i = pl.program_id(1)
    a = a_ref[...]
    p = _prev_rows(i, tpb, prev_ref, carry_ref)
    o_ref[...] = (_silu(_conv3(a, p, w_ref[...])) * g_ref[...]).astype(o_ref.dtype)
    st_ref[...] = a[tm - 2:tm]
    if tpb > 1:
        carry_ref[...] = a[tm - 8:tm]


def _ffn_act(zu, conv_w, l, prev, tm, tpb, tn):
    m = zu.shape[0]
    dff = prev.shape[-1]
    nb = prev.shape[0]
    nj = dff // tn
    return pl.pallas_call(
        functools.partial(_ffn_act_kernel, tm=tm, tpb=tpb),
        grid=(nj, m // tm),
        in_specs=[
            pl.BlockSpec((tm, tn), lambda j, i: (i, j)),
            pl.BlockSpec((tm, tn), lambda j, i: (i, j + nj)),
            pl.BlockSpec((None, 3, tn), lambda j, i: (l, 0, j)),
            pl.BlockSpec((None, 2, tn), lambda j, i: (i // tpb, 0, j)),
        ],
        out_specs=[
            pl.BlockSpec((tm, tn), lambda j, i: (i, j)),
            pl.BlockSpec((None, 2, tn), lambda j, i: (i // tpb, 0, j)),
        ],
        out_shape=[jax.ShapeDtypeStruct((m, dff), BF), jax.ShapeDtypeStruct((nb, 2, dff), F32)],
        scratch_shapes=[pltpu.VMEM((8, tn), F32)],
        compiler_params=_params("arbitrary", "arbitrary"),
        name="ffn_act",
    )(zu, zu, conv_w, prev)


def _ffn_up_kernel(h_ref, wa_ref, wg_ref, cw_ref, prev_ref, o_ref, st_ref, wab_ref, wgb_ref, carry_ref, *, tm, tpb):
    i = pl.program_id(1)

    @pl.when(i == 0)
    def _():
        wab_ref[...] = wa_ref[...].astype(BF)
        wgb_ref[...] = wg_ref[...].astype(BF)

    h = h_ref[...]
    a = jnp.dot(h, wab_ref[...], preferred_element_type=F32)
    g = jnp.dot(h, wgb_ref[...], preferred_element_type=F32)
    p = _prev_rows(i, tpb, prev_ref, carry_ref)
    o_ref[...] = (_silu(_conv3(a, p, cw_ref[...])) * g).astype(o_ref.dtype)
    st_ref[...] = a[tm - 2:tm]
    if tpb > 1:
        carry_ref[...] = a[tm - 8:tm]


def _ffn_up(h, w_up, conv_w, l, prev, tm, tpb, tn):
    m, d = h.shape
    dff = prev.shape[-1]
    nb = prev.shape[0]
    nj = dff // tn
    return pl.pallas_call(
        functools.partial(_ffn_up_kernel, tm=tm, tpb=tpb),
        grid=(nj, m // tm),
        in_specs=[
            pl.BlockSpec((tm, d), lambda j, i: (i, 0)),
            pl.BlockSpec((None, d, tn), lambda j, i: (l, 0, j)),
            pl.BlockSpec((None, d, tn), lambda j, i: (l, 0, j + nj)),
            pl.BlockSpec((None, 3, tn), lambda j, i: (l, 0, j)),
            pl.BlockSpec((None, 2, tn), lambda j, i: (i // tpb, 0, j)),
        ],
        out_specs=[
            pl.BlockSpec((tm, tn), lambda j, i: (i, j)),
            pl.BlockSpec((None, 2, tn), lambda j, i: (i // tpb, 0, j)),
        ],
        out_shape=[jax.ShapeDtypeStruct((m, dff), BF), jax.ShapeDtypeStruct((nb, 2, dff), F32)],
        scratch_shapes=[pltpu.VMEM((d, tn), BF), pltpu.VMEM((d, tn), BF), pltpu.VMEM((8, tn), F32)],
        compiler_params=_params("arbitrary", "arbitrary"),
        name="ffn_up_fused",
    )(h, w_up, w_up, conv_w, prev)


def _odd_post_kernel(ca_ref, cg_ref, du_ref, dv_ref, cw_ref, cb_ref, clg_ref, clb_ref, dlg_ref,
                     dlb_ref, ws_ref, bst_ref, prev_ref, o_ref, st_ref, v_ref, cbuf, *, tm, tpb):
    i = pl.program_id(0)
    cw = cw_ref.shape[-1]
    c = ca_ref[...] * _sigmoid(cg_ref[...])
    if tpb == 1:
        cbuf[2:32, :] = prev_ref[...]
    else:
        @pl.when(i % tpb == 0)
        def _():
            cbuf[2:32, :] = prev_ref[...]

        @pl.when(i % tpb != 0)
        def _():
            cbuf[0:32, :] = cbuf[tm:tm + 32, :]
    cbuf[32:32 + tm, :] = c
    acc = cw_ref[0:1, :] * cbuf[2:2 + tm, :]
    for k in range(1, CCONV_W):
        acc = acc + cw_ref[k:k + 1, :] * cbuf[2 + k:2 + k + tm, :]
    st_ref[...] = cbuf[tm + 2:tm + 32, :]
    o_ref[:, 0:cw] = _silu(_layernorm(acc + cb_ref[...], clg_ref[...], clb_ref[...])).astype(o_ref.dtype)

    u = _gelu_tanh(du_ref[...])
    v = _layernorm(_gelu_tanh(dv_ref[...]), dlg_ref[...], dlb_ref[...])
    v_ref[...] = v
    gw = cw // D_GROUPS
    trow = lax.broadcasted_iota(jnp.int32, (D_CHUNK, D_CHUNK), 0)
    tcol = lax.broadcasted_iota(jnp.int32, (D_CHUNK, D_CHUNK), 1)
    rows = min(tm, D_CHUNK)
    for ch in range(max(1, tm // D_CHUNK)):
        r0 = ch * D_CHUNK
        vch = v[r0:r0 + rows]
        if rows < D_CHUNK:
            vch = jnp.concatenate([vch, jnp.zeros((D_CHUNK - rows, cw), F32)], axis=0)
        vch = vch.astype(BF)
        for g in range(D_GROUPS):
            wg = jnp.where(tcol <= trow, ws_ref[g], 0.0).astype(BF)
            zz = jnp.dot(wg, vch[:, g * gw:(g + 1) * gw], preferred_element_type=F32)
            zz = zz + bst_ref[:, g:g + 1]
            o_ref[r0:r0 + rows, cw + g * gw:cw + (g + 1) * gw] = (
                u[r0:r0 + rows, g * gw:(g + 1) * gw] * zz[0:rows]).astype(o_ref.dtype)


def _odd_post(z, i_odd, prev, cconv_w, cconv_b, c_ln_g, c_ln_b, d_ln_g, d_ln_b, d_ws, d_bs_t, tm, tpb):
    m = z.shape[0]
    c = prev.shape[-1]
    nb = prev.shape[0]
    vec = lambda: pl.BlockSpec((None, 1, c), lambda i: (i_odd, 0, 0))
    r3 = lambda a: a.reshape(a.shape[0], 1, c)
    return pl.pallas_call(
        functools.partial(_odd_post_kernel, tm=tm, tpb=tpb),
        grid=(m // tm,),
        in_specs=[
            pl.BlockSpec((tm, c), lambda i: (i, 0)),
            pl.BlockSpec((tm, c), lambda i: (i, 1)),
            pl.BlockSpec((tm, c), lambda i: (i, 2)),
            pl.BlockSpec((tm, c), lambda i: (i, 3)),
            pl.BlockSpec((None, CCONV_W, c), lambda i: (i_odd, 0, 0)),
            vec(), vec(), vec(), vec(), vec(),
            pl.BlockSpec((None, D_GROUPS, D_CHUNK, D_CHUNK), lambda i: (i_odd, 0, 0, 0)),
            pl.BlockSpec((None, D_CHUNK, D_GROUPS), lambda i: (i_odd, 0, 0)),
            pl.BlockSpec((None, CCONV_W - 1, c), lambda i: (i // tpb, 0, 0)),
        ],
        out_specs=[
            pl.BlockSpec((tm, 2 * c), lambda i: (i, 0)),
            pl.BlockSpec((None, CCONV_W - 1, c), lambda i: (i // tpb, 0, 0)),
            pl.BlockSpec((tm, c), lambda i: (i, 0)),
        ],
        out_shape=[jax.ShapeDtypeStruct((m, 2 * c), BF),
                   jax.ShapeDtypeStruct((nb, CCONV_W - 1, c), F32),
                   jax.ShapeDtypeStruct((m, c), F32)],
        scratch_shapes=[pltpu.VMEM((32 + tm, c), F32)],
        compiler_params=_params("arbitrary"),
        name="odd_post",
    )(z, z, z, z, cconv_w, r3(cconv_b), r3(c_ln_g), r3(c_ln_b), r3(d_ln_g), r3(d_ln_b), d_ws, d_bs_t, prev)


def _rope128(x, cos, sin_signed):
    lane = lax.broadcasted_iota(jnp.int32, x.shape, 1)
    swapped = jnp.where((lane & (HEAD_DIM - 1)) < HEAD_DIM // 2,
                        pltpu.roll(x, LANES - HEAD_DIM // 2, 1), pltpu.roll(x, HEAD_DIM // 2, 1))
    return x * cos + swapped * sin_signed


def _rope_slab(ref, c0, width, cos, sin_signed):
    return jnp.concatenate(
        [_rope128(ref[:, c0 + k * LANES:c0 + (k + 1) * LANES], cos, sin_signed) for k in range(width // LANES)],
        axis=1)


def _rope_prompt_kernel(zq_ref, zn_ref, zw_ref, cos_ref, sin_ref, qt_ref, nsat_ref, wint_ref, kv_ref, kc_ref,
                        vc_ref, *, ts):
    cos, sin = cos_ref[...], sin_ref[...]
    kvw = KV_HEADS * HEAD_DIM
    scale = HEAD_DIM ** -0.5
    for k in range(N_HEADS * HEAD_DIM // LANES):
        rt = (_rope128(zq_ref[:, k * LANES:(k + 1) * LANES], cos, sin) * scale).T
        qt_ref[2 * k] = rt[:HEAD_DIM].astype(qt_ref.dtype)
        qt_ref[2 * k + 1] = rt[HEAD_DIM:].astype(qt_ref.dtype)
    k_cmp = _rope_slab(zn_ref, 0, kvw, cos, sin)
    v_cmp = zn_ref[:, kvw:2 * kvw]
    k_sel = _rope_slab(zn_ref, 2 * kvw, kvw, cos, sin)
    v_sel = zn_ref[:, 3 * kvw:4 * kvw]
    k_win = _rope_slab(zw_ref, 0, kvw, cos, sin)
    v_win = zw_ref[:, kvw:2 * kvw]
    for t, slab in enumerate((k_cmp, v_cmp, k_sel, v_sel)):
        nsat_ref[t] = slab.T.reshape(KV_HEADS, HEAD_DIM, ts)
    for t, slab in enumerate((k_win, v_win)):
        wint_ref[t] = slab.T.reshape(KV_HEADS, HEAD_DIM, ts)
    for t, slab in enumerate((k_sel, v_sel, k_win, v_win)):
        for g in range(KV_HEADS):
            kv_ref[t, g] = slab[:, g * HEAD_DIM:(g + 1) * HEAD_DIM].astype(kv_ref.dtype)
    nblk = ts // NSA_BLOCK
    kc = jnp.sum(k_cmp.reshape(nblk, NSA_BLOCK, kvw), axis=1) * (1.0 / NSA_BLOCK)
    vc = jnp.sum(v_cmp.reshape(nblk, NSA_BLOCK, kvw), axis=1) * (1.0 / NSA_BLOCK)
    for g in range(KV_HEADS):
        kc_ref[g] = kc[:, g * HEAD_DIM:(g + 1) * HEAD_DIM]
        vc_ref[g] = vc[:, g * HEAD_DIM:(g + 1) * HEAD_DIM]


def _rope_prompt(z, cos, sin, batch, seq, ts):
    tpb = seq // ts
    qw = N_HEADS * HEAD_DIM
    nblk = ts // NSA_BLOCK
    return pl.pallas_call(
        functools.partial(_rope_prompt_kernel, ts=ts),
        grid=(batch * tpb,),
        in_specs=[
            pl.BlockSpec((ts, qw), lambda i: (i, 3)),
            pl.BlockSpec((ts, qw), lambda i: (i, 4)),
            pl.BlockSpec((ts, qw // 2), lambda i: (i, 10)),
            pl.BlockSpec((ts, LANES), lambda i: (i % tpb, 0)),
            pl.BlockSpec((ts, LANES), lambda i: (i % tpb, 0)),
        ],
        out_specs=[
            pl.BlockSpec((None, N_HEADS, HEAD_DIM, ts), lambda i: (i // tpb, 0, 0, i % tpb)),
            pl.BlockSpec((None, 4, KV_HEADS, HEAD_DIM, ts), lambda i: (i // tpb, 0, 0, 0, i % tpb)),
            pl.BlockSpec((None, 2, KV_HEADS, HEAD_DIM, ts), lambda i: (i // tpb, 0, 0, 0, i % tpb)),
            pl.BlockSpec((None, 4, KV_HEADS, ts, HEAD_DIM), lambda i: (i // tpb, 0, 0, i % tpb, 0)),
            pl.BlockSpec((None, KV_HEADS, nblk, HEAD_DIM), lambda i: (i // tpb, 0, i % tpb, 0)),
            pl.BlockSpec((None, KV_HEADS, nblk, HEAD_DIM), lambda i: (i // tpb, 0, i % tpb, 0)),
        ],
        out_shape=[
            jax.ShapeDtypeStruct((batch, N_HEADS, HEAD_DIM, seq), BF),
            jax.ShapeDtypeStruct((batch, 4, KV_HEADS, HEAD_DIM, seq), F32),
            jax.ShapeDtypeStruct((batch, 2, KV_HEADS, HEAD_DIM, seq), F32),
            jax.ShapeDtypeStruct((batch, 4, KV_HEADS, seq, HEAD_DIM), BF),
            jax.ShapeDtypeStruct((batch, KV_HEADS, seq // NSA_BLOCK, HEAD_DIM), F32),
            jax.ShapeDtypeStruct((batch, KV_HEADS, seq // NSA_BLOCK, HEAD_DIM), F32),
        ],
        compiler_params=_params("arbitrary"),
        name="rope_prompt",
    )(z, z, z, cos, sin)


def _rope_sample_kernel(zq_ref, zn_ref, zw_ref, cos_ref, sin_ref, q_ref, nsa_ref, win_ref):
    cos, sin = cos_ref[...], sin_ref[...]
    kvw = KV_HEADS * HEAD_DIM
    scale = HEAD_DIM ** -0.5
    for k in range(N_HEADS * HEAD_DIM // LANES):
        r = _rope128(zq_ref[:, k * LANES:(k + 1) * LANES], cos, sin) * scale
        q_ref[2 * k] = r[:, :HEAD_DIM]
        q_ref[2 * k + 1] = r[:, HEAD_DIM:]
    nsa_ref[:, 0:kvw] = _rope_slab(zn_ref, 0, kvw, cos, sin)
    nsa_ref[:, kvw:2 * kvw] = zn_ref[:, kvw:2 * kvw]
    nsa_ref[:, 2 * kvw:3 * kvw] = _rope_slab(zn_ref, 2 * kvw, kvw, cos, sin)
    nsa_ref[:, 3 * kvw:4 * kvw] = zn_ref[:, 3 * kvw:4 * kvw]
    win_ref[:, 0:kvw] = _rope_slab(zw_ref, 0, kvw, cos, sin)
    win_ref[:, kvw:2 * kvw] = zw_ref[:, kvw:2 * kvw]


def _rope_sample(z, cos, sin, batch, seq):
    qw = N_HEADS * HEAD_DIM
    return pl.pallas_call(
        _rope_sample_kernel,
        grid=(batch,),
        in_specs=[
            pl.BlockSpec((seq, qw), lambda i: (i, 3)),
            pl.BlockSpec((seq, qw), lambda i: (i, 4)),
            pl.BlockSpec((seq, qw // 2), lambda i: (i, 10)),
            pl.BlockSpec((seq, LANES), lambda i: (0, 0)),
            pl.BlockSpec((seq, LANES), lambda i: (0, 0)),
        ],
        out_specs=[
            pl.BlockSpec((None, N_HEADS, seq, HEAD_DIM), lambda i: (i, 0, 0, 0)),
            pl.BlockSpec((seq, qw), lambda i: (i, 0)),
            pl.BlockSpec((seq, qw // 2), lambda i: (i, 0)),
        ],
        out_shape=[
            jax.ShapeDtypeStruct((batch, N_HEADS, seq, HEAD_DIM), F32),
            jax.ShapeDtypeStruct((batch * seq, qw), F32),
            jax.ShapeDtypeStruct((batch * seq, qw // 2), F32),
        ],
        compiler_params=_params("arbitrary"),
        name="rope_sample",
    )(z, z, z, cos, sin)


def _cmp_branch(kc, vc, q, qpos, nq):
    nb = kc.shape[0]
    st = lax.dot_general(kc.astype(BF), q, (((1,), (1,)), ((), ())), preferred_element_type=F32)
    blk = lax.broadcasted_iota(jnp.int32, st.shape, 0)
    ok = (blk + 1) * NSA_BLOCK <= qpos + 1
    sm = jnp.where(ok, st, NEG)
    mx = jnp.max(sm, axis=0, keepdims=True)
    e = jnp.where(ok, jnp.exp(sm - mx), 0.0)
    den = jnp.sum(e, axis=0, keepdims=True)
    pt = e / jnp.where(den > 0.0, den, 1.0)
    o_cmp = lax.dot_general(pt.astype(BF), vc.astype(BF), (((0,), (0,)), ((), ())), preferred_element_type=F32)
    imp = pt[:, 0:nq]
    for r in range(1, GQA):
        imp = imp + pt[:, r * nq:(r + 1) * nq]
    return o_cmp, imp


def _importance(imp, qpos_q):
    blk = lax.broadcasted_iota(jnp.int32, imp.shape, 0)
    cur = qpos_q // NSA_BLOCK
    forced = (blk == 0) | (blk == cur) | (blk == cur - 1)
    imp = jnp.where(forced, GQA + 1.0, imp)
    return jnp.where(blk <= cur, imp, -1.0)


def _select_topk(imp_ref, nb):
    imp = imp_ref[...]
    blk = lax.broadcasted_iota(jnp.int32, imp.shape, 0)

    def body(i, rank):
        row = imp_ref[pl.ds(i, 1), :]
        ahead = (row > imp) | ((row == imp) & (i < blk))
        return rank + jnp.where(ahead, 1.0, 0.0)

    rank = lax.fori_loop(0, nb, body, jnp.zeros(imp.shape, F32))
    return jnp.where(rank < float(N_SEL), 1.0, 0.0)


def _expand_blocks(sel_t, n_keys, first_block):
    nbl = sel_t.shape[0]
    kb = lax.broadcasted_iota(jnp.int32, (nbl, n_keys), 1) // NSA_BLOCK + first_block
    nn = lax.broadcasted_iota(jnp.int32, (nbl, n_keys), 0)
    e = jnp.where(kb == nn, 1.0, 0.0).astype(BF)
    return lax.dot_general(sel_t.astype(BF), e, (((0,), (0,)), ((), ())), preferred_element_type=F32)


def _online_update(s, valid, vt, m_ref, l_ref, acc_ref, idx, v_rows=None):
    nk = s.shape[1]
    s = jnp.where(valid, s, NEG)
    m_prev = m_ref[idx]
    m_new = jnp.maximum(m_prev, jnp.max(s, axis=1, keepdims=True))
    alpha = jnp.exp(m_prev - m_new)
    p = jnp.where(valid, jnp.exp(s - jnp.concatenate([m_new] * (nk // LANES), axis=1)), 0.0)
    l_ref[idx] = alpha * l_ref[idx] + jnp.sum(p, axis=1, keepdims=True)
    if v_rows is None:
        pv = lax.dot_general(p.astype(BF), vt, (((1,), (1,)), ((), ())), preferred_element_type=F32)
    else:
        pv = jnp.dot(p.astype(BF), v_rows, preferred_element_type=F32)
    acc_ref[idx] = alpha[:, :HEAD_DIM] * acc_ref[idx] + pv
    m_ref[idx] = m_new


def _online_update_t(k_rows, qt, bias, v_rows, m_ref, l_ref, acc_ref, idx):
    st = jnp.dot(k_rows, qt, preferred_element_type=F32)
    if bias is not None:
        nq = bias.shape[1]
        st = jnp.concatenate([st[:, r * nq:(r + 1) * nq] + bias for r in range(GQA)], axis=1)
    m_prev = m_ref[idx]
    m_new = jnp.maximum(m_prev, jnp.max(st, axis=0, keepdims=True))
    alpha = jnp.exp(m_prev - m_new)
    p = jnp.exp(st - m_new)
    l_ref[idx] = alpha * l_ref[idx] + jnp.sum(p, axis=0, keepdims=True)
    pv = lax.dot_general(v_rows, p.astype(BF), (((0,), (0,)), ((), ())), preferred_element_type=F32)
    acc_ref[idx] = alpha * acc_ref[idx] + pv
    m_ref[idx] = m_new


def _nsa_prompt_kernel(qt_ref, kc_ref, vc_ref, ks_ref, vs_ref, kw_ref, vw_ref, gl_ref, o_ref,
                       imp_ref, m_ref, l_ref, acc_ref, *, tq, seq):
    qi = pl.program_id(2)
    q0 = qi * tq
    rq = GQA * tq
    nb = seq // NSA_BLOCK
    bpt = tq // NSA_BLOCK
    qt = jnp.concatenate([qt_ref[r] for r in range(GQA)], axis=1)

    st = jnp.dot(kc_ref[...].astype(BF), qt, preferred_element_type=F32)
    blk = lax.broadcasted_iota(jnp.int32, (nb, rq), 0)
    qpos_r = q0 + (lax.broadcasted_iota(jnp.int32, (nb, rq), 1) & (tq - 1))
    ok = (blk + 1) * NSA_BLOCK <= qpos_r + 1
    sm = jnp.where(ok, st, NEG)
    e = jnp.where(ok, jnp.exp(sm - jnp.max(sm, axis=0, keepdims=True)), 0.0)
    den = jnp.sum(e, axis=0, keepdims=True)
    pt = e / jnp.where(den > 0.0, den, 1.0)
    o_cmp = lax.dot_general(vc_ref[...].astype(BF), pt.astype(BF), (((0,), (0,)), ((), ())),
                            preferred_element_type=F32)
    imp = pt[:, 0:tq]
    for r in range(1, GQA):
        imp = imp + pt[:, r * tq:(r + 1) * tq]
    imp_ref[...] = _importance(imp, q0 + lax.broadcasted_iota(jnp.int32, (nb, tq), 1))
    sel_bf = _select_topk(imp_ref, nb).astype(BF)

    m_ref[...] = jnp.full(m_ref.shape, NEG, F32)
    l_ref[...] = jnp.zeros(l_ref.shape, F32)
    acc_ref[...] = jnp.zeros(acc_ref.shape, F32)

    krow = lax.broadcasted_iota(jnp.int32, (tq, tq), 0)
    qcol = lax.broadcasted_iota(jnp.int32, (tq, tq), 1)
    e_rows = lax.broadcasted_iota(jnp.int32, (tq, nb), 0) // NSA_BLOCK
    e_cols = lax.broadcasted_iota(jnp.int32, (tq, nb), 1)

    def sel_bias(c):
        e = jnp.where(e_rows + c * bpt == e_cols, 1.0, 0.0).astype(BF)
        return (jnp.dot(e, sel_bf, preferred_element_type=F32) - 1.0) * (-NEG)

    def rows_of(c):
        return pl.ds(pl.multiple_of(c * tq, tq), tq)

    def sel_chunk(c):
        _online_update_t(ks_ref[rows_of(c), :], qt, sel_bias(c), vs_ref[rows_of(c), :], m_ref, l_ref, acc_ref, 0)

    def sel_pair(j, carry):
        sel_chunk(2 * j)
        sel_chunk(2 * j + 1)
        return carry

    lax.fori_loop(0, qi // 2, sel_pair, 0)

    @pl.when(qi % 2 == 1)
    def _():
        sel_chunk(qi - 1)
    causal = krow <= qcol
    _online_update_t(ks_ref[rows_of(qi), :], qt, jnp.where(causal, sel_bias(qi), NEG), vs_ref[rows_of(qi), :],
                     m_ref, l_ref, acc_ref, 0)
    _online_update_t(kw_ref[rows_of(qi), :], qt, jnp.where(causal, 0.0, NEG), vw_ref[rows_of(qi), :],
                     m_ref, l_ref, acc_ref, 1)
    n_full = (WINDOW - tq) // tq
    for rel in range(1, n_full + 2):
        @pl.when(qi >= rel)
        def _(rel=rel):
            c = qi - rel
            bias = None if rel <= n_full else jnp.where(krow > qcol + (rel * tq - WINDOW), 0.0, NEG)
            _online_update_t(kw_ref[rows_of(c), :], qt, bias, vw_ref[rows_of(c), :], m_ref, l_ref, acc_ref, 1)

    o_sel = acc_ref[0] * (1.0 / l_ref[0])
    o_win = acc_ref[1] * (1.0 / l_ref[1])
    gate_t = _sigmoid(gl_ref[...]).T
    outs = []
    for r in range(GQA):
        cols = slice(r * tq, (r + 1) * tq)
        o_t = (gate_t[3 * r:3 * r + 1] * o_cmp[:, cols] + gate_t[3 * r + 1:3 * r + 2] * o_sel[:, cols]
               + gate_t[3 * r + 2:3 * r + 3] * o_win[:, cols])
        outs.append(o_t.T)
    o_ref[...] = jnp.concatenate(outs, axis=1).astype(o_ref.dtype)


def _nsa_prompt(qt, kc, vc, kv, gl, batch, seq, tq):
    nq = seq // tq
    nb = seq // NSA_BLOCK
    rq = GQA * tq
    assert WINDOW % tq == 0
    kv_spec = lambda t: pl.BlockSpec((None, None, None, seq, HEAD_DIM), lambda b, g, i: (b, t, g, 0, 0))
    cmp_spec = pl.BlockSpec((None, None, nb, HEAD_DIM), lambda b, g, i: (b, g, 0, 0))
    return pl.pallas_call(
        functools.partial(_nsa_prompt_kernel, tq=tq, seq=seq),
        grid=(batch, KV_HEADS, nq),
        in_specs=[
            pl.BlockSpec((None, GQA, HEAD_DIM, tq), lambda b, g, i: (b, g, 0, i)),
            cmp_spec, cmp_spec,
            kv_spec(0), kv_spec(1), kv_spec(2), kv_spec(3),
            pl.BlockSpec((tq, LANES), lambda b, g, i: (b * nq + i, g)),
        ],
        out_specs=pl.BlockSpec((tq, GQA * HEAD_DIM), lambda b, g, i: (b * nq + i, g)),
        out_shape=jax.ShapeDtypeStruct((batch * seq, N_HEADS * HEAD_DIM), BF),
        scratch_shapes=[
            pltpu.VMEM((nb, tq), F32),
            pltpu.VMEM((2, 1, rq), F32),
            pltpu.VMEM((2, 1, rq), F32),
            pltpu.VMEM((2, HEAD_DIM, rq), F32),
        ],
        compiler_params=_params("arbitrary", "arbitrary", "arbitrary"),
        name="nsa_prompt",
    )(qt, kc, vc, kv, kv, kv, kv, gl)


def _cmp_means_kernel(pt_ref, *refs, n_pages):
    page_refs = refs[:n_pages]
    kc_ref, vc_ref = refs[n_pages], refs[n_pages + 1]
    bpp = PAGE_SIZE // NSA_BLOCK
    kvw = KV_HEADS * HEAD_DIM
    for t, out in enumerate((kc_ref, vc_ref)):
        x = jnp.concatenate([page_refs[p][t].reshape(kvw, PAGE_SIZE).T for p in range(n_pages)], axis=0)
        out[...] = jnp.sum(x.reshape(n_pages * bpp, NSA_BLOCK, kvw), axis=1) * (1.0 / NSA_BLOCK)


def _cmp_means(cache_t, layer, page_table, n_pages):
    batch, ppb = page_table.shape
    kvw = KV_HEADS * HEAD_DIM
    bpp = PAGE_SIZE // NSA_BLOCK
    steps = ppb // n_pages

    def page_spec(p):
        return pl.BlockSpec((None, None, 2, KV_HEADS, HEAD_DIM, PAGE_SIZE),
                            lambda b, s, pt: (layer, pt[b, s * n_pages + p], 0, 0, 0, 0))

    out_spec = pl.BlockSpec((None, n_pages * bpp, kvw), lambda b, s, pt: (b, s, 0))
    return pl.pallas_call(
        functools.partial(_cmp_means_kernel, n_pages=n_pages),
        grid_spec=pltpu.PrefetchScalarGridSpec(
            num_scalar_prefetch=1, grid=(batch, steps),
            in_specs=[page_spec(p) for p in range(n_pages)],
            out_specs=[out_spec, out_spec]),
        out_shape=[jax.ShapeDtypeStruct((batch, ppb * bpp, kvw), F32)] * 2,
        compiler_params=_params("arbitrary", "arbitrary"),
        name="cmp_means",
    )(page_table, *([cache_t] * n_pages))


def _nsa_sample_kernel(pt_ref, *refs, n_pages, past, s_new):
    (q_ref, kc_ref, vc_ref, new_ref, wnew_ref, wbuf_ref, gl_ref) = refs[:7]
    page_refs = refs[7:7 + n_pages]
    o_ref = refs[7 + n_pages]
    kcf, vcf, imp_ref, sel_ref, m_ref, l_ref, acc_ref, ocmp_ref = refs[8 + n_pages:]
    step = pl.program_id(1)
    nsteps = pl.num_programs(1)
    kvw = KV_HEADS * HEAD_DIM
    rq = GQA * s_new
    nbp = past // NSA_BLOCK
    nbf = kcf.shape[0]
    wlen = wbuf_ref.shape[-1]
    pad_rows = LANES - s_new

    def q_of(g):
        return q_ref[g * GQA:(g + 1) * GQA].reshape(rq, HEAD_DIM).astype(BF)

    def pad_keys(x):
        return jnp.concatenate([x, jnp.zeros((pad_rows, HEAD_DIM), F32)], axis=0).astype(BF)

    @pl.when(step == 0)
    def _():
        row8 = lax.broadcasted_iota(jnp.int32, (nbf - nbp, kvw), 0)
        for full, src, c0 in ((kcf, kc_ref, 0), (vcf, vc_ref, kvw)):
            full[0:nbp, :] = src[...]
            mean_new = jnp.sum(new_ref[:, c0:c0 + kvw], axis=0, keepdims=True) * (1.0 / NSA_BLOCK)
            full[nbp:nbf, :] = jnp.where(row8 == 0, mean_new, 0.0)
        col = lax.broadcasted_iota(jnp.int32, (nbf, rq), 1)
        qpos_q = past + lax.broadcasted_iota(jnp.int32, (nbf, s_new), 1)
        for g in range(KV_HEADS):
            lanes = slice(g * HEAD_DIM, (g + 1) * HEAD_DIM)
            o_cmp, imp = _cmp_branch(kcf[:, lanes], vcf[:, lanes], q_of(g), past + (col & (s_new - 1)), s_new)
            ocmp_ref[g] = o_cmp
            imp_ref[:, g * s_new:(g + 1) * s_new] = _importance(imp, qpos_q)
        sel = _select_topk(imp_ref, nbf)
        for g in range(KV_HEADS):
            sel_ref[g] = jnp.concatenate([sel[:, g * s_new:(g + 1) * s_new]] * GQA, axis=1)
        m_ref[...] = jnp.full(m_ref.shape, NEG, F32)
        l_ref[...] = jnp.zeros(l_ref.shape, F32)
        acc_ref[...] = jnp.zeros(acc_ref.shape, F32)

    nk = n_pages * PAGE_SIZE
    nbl = nk // NSA_BLOCK
    for g in range(KV_HEADS):
        kt = jnp.concatenate([page_refs[p][0, g] for p in range(n_pages)], axis=1).astype(BF)
        vt = jnp.concatenate([page_refs[p][1, g] for p in range(n_pages)], axis=1).astype(BF)
        sel_rows = sel_ref[g, pl.ds(pl.multiple_of(step * nbl, nbl), nbl), :]
        valid = _expand_blocks(sel_rows, nk, 0) > 0.5
        s = jnp.dot(q_of(g), kt, preferred_element_type=F32)
        _online_update(s, valid, vt, m_ref, l_ref, acc_ref, g)

    @pl.when(step == nsteps - 1)
    def _():
        tq_col = lax.broadcasted_iota(jnp.int32, (rq, LANES), 0) & (s_new - 1)
        tk = lax.broadcasted_iota(jnp.int32, (rq, LANES), 1)
        new_ok = (tk <= tq_col) & (tk < s_new)
        gate = _sigmoid(gl_ref[...])
        tail = nbf - 16
        for g in range(KV_HEADS):
            q = q_of(g)
            lane0 = 2 * kvw + g * HEAD_DIM
            k_new = pad_keys(new_ref[:, lane0:lane0 + HEAD_DIM])
            v_new = pad_keys(new_ref[:, lane0 + kvw:lane0 + kvw + HEAD_DIM])
            kb = lax.broadcasted_iota(jnp.int32, (16, LANES), 0)
            e = jnp.where(kb == nbp - tail, 1.0, 0.0).astype(BF)
            selx = lax.dot_general(sel_ref[g, tail:nbf, :].astype(BF), e, (((0,), (0,)), ((), ())),
                                   preferred_element_type=F32)
            s = lax.dot_general(q, k_new, (((1,), (1,)), ((), ())), preferred_element_type=F32)
            _online_update(s, (selx > 0.5) & new_ok, None, m_ref, l_ref, acc_ref, g, v_rows=v_new)
            o_sel = acc_ref[g] / l_ref[g][:, :HEAD_DIM]
            wl0 = g * HEAD_DIM
            kw_new = pad_keys(wnew_ref[:, wl0:wl0 + HEAD_DIM])
            vw_new = pad_keys(wnew_ref[:, kvw + wl0:kvw + wl0 + HEAD_DIM])
            sb = jnp.dot(q, wbuf_ref[0, g].astype(BF), preferred_element_type=F32)
            sn = lax.dot_general(q, kw_new, (((1,), (1,)), ((), ())), preferred_element_type=F32)
            jb = lax.broadcasted_iota(jnp.int32, (rq, wlen), 1)
            tq_b = lax.broadcasted_iota(jnp.int32, (rq, wlen), 0) & (s_new - 1)
            ok_b = (past - wlen + jb > past + tq_b - WINDOW) & (past - wlen + jb >= 0)
            sb = jnp.where(ok_b, sb, NEG)
            sn = jnp.where(new_ok, sn, NEG)
            mx = jnp.maximum(jnp.max(sb, axis=1, keepdims=True), jnp.max(sn, axis=1, keepdims=True))
            pb = jnp.where(ok_b, jnp.exp(sb - mx), 0.0)
            pn = jnp.where(new_ok, jnp.exp(sn - mx), 0.0)
            den = jnp.sum(pb, axis=1, keepdims=True) + jnp.sum(pn, axis=1, keepdims=True)
            o_win = (lax.dot_general(pb.astype(BF), wbuf_ref[1, g].astype(BF), (((1,), (1,)), ((), ())),
                                     preferred_element_type=F32)
                     + jnp.dot(pn.astype(BF), vw_new, preferred_element_type=F32)) / den
            o_cmp = ocmp_ref[g]
            for r in range(GQA):
                rows = slice(r * s_new, (r + 1) * s_new)
                c = g * LANES + 3 * r
                h = g * GQA + r
                o_ref[:, h * HEAD_DIM:(h + 1) * HEAD_DIM] = (
                    gate[:, c:c + 1] * o_cmp[rows] + gate[:, c + 1:c + 2] * o_sel[rows]
                    + gate[:, c + 2:c + 3] * o_win[rows]).astype(o_ref.dtype)


def _nsa_sample(cache_t, layer, page_table, q, kc, vc, nsa_new, win_new, wbuf_t, gl, n_pages, past, s_new):
    batch, ppb = page_table.shape
    kvw = KV_HEADS * HEAD_DIM
    qw = N_HEADS * HEAD_DIM
    nbp = past // NSA_BLOCK
    nbf = nbp + 8
    wlen = wbuf_t.shape[-1]
    rq = GQA * s_new
    steps = ppb // n_pages

    def page_spec(p):
        return pl.BlockSpec((None, None, 2, KV_HEADS, HEAD_DIM, PAGE_SIZE),
                            lambda b, s, pt: (layer, pt[b, s * n_pages + p], 1, 0, 0, 0))

    per_b = lambda shape: pl.BlockSpec((None,) + shape, lambda b, s, pt: (b,) + (0,) * len(shape))
    rows_b = lambda w: pl.BlockSpec((s_new, w), lambda b, s, pt: (b, 0))
    return pl.pallas_call(
        functools.partial(_nsa_sample_kernel, n_pages=n_pages, past=past, s_new=s_new),
        grid_spec=pltpu.PrefetchScalarGridSpec(
            num_scalar_prefetch=1, grid=(batch, steps),
            in_specs=[
                per_b((N_HEADS, s_new, HEAD_DIM)),
                per_b((nbp, kvw)), per_b((nbp, kvw)),
                rows_b(4 * kvw), rows_b(2 * kvw),
                pl.BlockSpec((None, None, 2, KV_HEADS, HEAD_DIM, wlen), lambda b, s, pt: (layer, b, 0, 0, 0, 0)),
                rows_b(KV_HEADS * LANES),
            ] + [page_spec(p) for p in range(n_pages)],
            out_specs=rows_b(qw),
            scratch_shapes=[
                pltpu.VMEM((nbf, kvw), F32), pltpu.VMEM((nbf, kvw), F32),
                pltpu.VMEM((nbf, KV_HEADS * s_new), F32),
                pltpu.VMEM((KV_HEADS, nbf, rq), F32),
                pltpu.VMEM((KV_HEADS, rq, LANES), F32),
                pltpu.VMEM((KV_HEADS, rq, LANES), F32),
                pltpu.VMEM((KV_HEADS, rq, HEAD_DIM), F32),
                pltpu.VMEM((KV_HEADS, rq, HEAD_DIM), F32),
            ]),
        out_shape=jax.ShapeDtypeStruct((batch * s_new, qw), BF),
        compiler_params=_params("arbitrary", "arbitrary"),
        name="nsa_sample",
    )(page_table, q, kc, vc, nsa_new, win_new, wbuf_t, gl, *([cache_t] * n_pages))


def _rope_tables(pos):
    half = HEAD_DIM // 2
    freq = ROPE_THETA ** (-jnp.arange(half, dtype=F32) / half)
    ang = pos.astype(F32)[:, None] * freq[None, :]
    cos, sin = jnp.cos(ang), jnp.sin(ang)
    return jnp.tile(cos, (1, LANES // half)), jnp.tile(jnp.concatenate([-sin, sin], axis=1), (1, LANES // HEAD_DIM))


def _trunk(x, mod5, b_off, batch, seq, pos, sconv_prev, cconv_prev, ffn_prev, nsa_fn, tm, mm_tm, gate_of, wts):
    (g_mix, g_ffn, g_final, w_in_even_t, w_gate_t, sconv_w, w_out_even, w_in_odd, cconv_w, cconv_b, c_ln_g,
     c_ln_b, d_ln_g, d_ln_b, d_ws, d_bs_t, w_out_odd, w_up, ffn_conv_w, w_down) = wts
    depth = g_mix.shape[0]
    d = x.shape[1]
    tpb = seq // tm
    a_w = sconv_w.shape[-1]
    dff = ffn_conv_w.shape[-1]
    qw = N_HEADS * HEAD_DIM
    kv_cols = 3 * a_w + qw + 6 * KV_HEADS * HEAD_DIM
    cos, sin = _rope_tables(pos)
    out_tn = 1024
    new_nsa, new_win, new_s, new_c, new_dv, new_f = [], [], [], [], [], []
    for l in range(depth):
        i = l // 2
        h = _norm_mod(x, g_mix, mod5, l, 0, b_off, tm, tpb)
        gate_spec, gate = gate_of(l, 2, mm_tm, out_tn)
        if l % 2 == 0:
            z = _mm([h], w_in_even_t, i, kv_cols, mm_tm, kv_cols // 4, wt=True, name="in_even")
            gl = _mm([h], w_gate_t, i, KV_HEADS * LANES, mm_tm, 512, wt=True, name="gate_logits")
            mix_a, sb = _mixer_a(z, sconv_w, i, sconv_prev[i], tm, tpb)
            o_b, nsa_rows, win_state = nsa_fn(i, z, gl, cos, sin)
            x = _mm([mix_a, o_b], w_out_even, i, d, mm_tm, out_tn, res=x, gate_spec=gate_spec, gate=gate,
                    name="out_even")
            new_s.append(sb)
            new_nsa.append(nsa_rows)
            new_win.append(win_state)
        else:
            z = _mm([h], w_in_odd, i, w_in_odd.shape[-1], mm_tm, out_tn, name="in_odd")
            mix, cb, v = _odd_post(z, i, cconv_prev[i], cconv_w, cconv_b, c_ln_g, c_ln_b, d_ln_g, d_ln_b,
                                   d_ws, d_bs_t, tm, tpb)
            x = _mm([mix], w_out_odd, i, d, mm_tm, out_tn, res=x, gate_spec=gate_spec, gate=gate, name="out_odd")
            new_c.append(cb)
            new_dv.append(v)
        h = _norm_mod(x, g_ffn, mod5, l, 3, b_off, tm, tpb)
        if seq % mm_tm == 0:
            act, fb = _ffn_up(h, w_up, ffn_conv_w, l, ffn_prev[l], mm_tm, seq // mm_tm, 512)
        else:
            zu = _mm([h], w_up, l, 2 * dff, mm_tm, 512, name="ffn_up")
            act, fb = _ffn_act(zu, ffn_conv_w, l, ffn_prev[l], tm, tpb, dff)
        down_tm = min(mm_tm, 512)
        gate_spec, gate = gate_of(l, 5, down_tm, 512)
        x = _mm([act], w_down, l, d, down_tm, 512, res=x, gate_spec=gate_spec, gate=gate, name="ffn_down")
        new_f.append(fb)
    y = _final_norm(x, g_final, tm)
    return y, new_nsa, new_win, new_s, new_c, new_dv, new_f


def kernel(x_prompt, x_sample, cache_nsa_kv, state_win_kv, state_sconv, state_cconv, state_ffn_conv, page_table, c_prompt, c_sample, g_mix, g_ffn, g_final, w_ada, b_ada, w_in_even, sconv_w, w_out_even, w_in_odd, cconv_w, cconv_b, c_ln_g, c_ln_b, d_ln_g, d_ln_b, d_ws, d_bs, w_out_odd, w_up, ffn_conv_w, w_down):
    bp, sp, d = x_prompt.shape
    bs, ss, _ = x_sample.shape
    depth = g_mix.shape[0]
    n_even = w_in_even.shape[0]
    past = page_table.shape[1] * PAGE_SIZE
    a_w = sconv_w.shape[-1]
    dff = ffn_conv_w.shape[-1]
    kvw = KV_HEADS * HEAD_DIM
    dt = x_prompt.dtype

    rows = -(-(bp + bs) // 8) * 8
    c_all = jnp.concatenate([c_prompt, c_sample, jnp.zeros((rows - bp - bs, d), dt)], axis=0)
    mod4 = _ada(c_all, w_ada, b_ada)
    mod5 = mod4.reshape(depth, 6, rows, 1, d)

    gate_c0 = 3 * a_w + N_HEADS * HEAD_DIM + 6 * kvw
    w_in_even_t = jnp.swapaxes(w_in_even, 1, 2)
    wg = w_in_even_t[:, gate_c0:, :].reshape(n_even, KV_HEADS, GQA * 3, d)
    w_gate_t = jnp.pad(wg, ((0, 0), (0, 0), (0, LANES - GQA * 3), (0, 0))).reshape(n_even, KV_HEADS * LANES, d)

    wts = (g_mix, g_ffn, g_final, w_in_even_t, w_gate_t, sconv_w, w_out_even, w_in_odd, cconv_w, cconv_b, c_ln_g,
           c_ln_b, d_ln_g, d_ln_b, d_ws, jnp.swapaxes(d_bs, 1, 2), w_out_odd, w_up, ffn_conv_w, w_down)

    tm_p = 512
    tq = 256

    def prompt_nsa(i, z, gl, cos, sin):
        qt, nsat, wint, kv, kc, vc = _rope_prompt(z, cos, sin, bp, sp, tm_p)
        o_b = _nsa_prompt(qt, kc, vc, kv, gl, bp, sp, tq)
        keep = min(WINDOW, sp)
        nsa_rows = jnp.transpose(nsat, (0, 4, 1, 2, 3))
        win_state = jnp.transpose(wint[..., sp - keep:], (0, 4, 1, 2, 3))
        return o_b, nsa_rows, win_state

    def prompt_gate(l, which, tm, tn):
        return (pl.BlockSpec((None, None, None, 1, tn), lambda j, i: (l, which, (i * tm) // sp, 0, j)), mod5)

    mm_tm_p = 1024
    zeros = lambda n, r, w: jnp.zeros((n, bp, r, w), dt)
    (y_p, nsa_p, win_p, s_p, c_p, _, f_p) = _trunk(
        x_prompt.reshape(bp * sp, d), mod5, 0, bp, sp, jnp.arange(sp, dtype=jnp.int32),
        zeros(n_even, 2, a_w), zeros(depth // 2, CCONV_W - 1, a_w), zeros(depth, 2, dff),
        prompt_nsa, tm_p, mm_tm_p, prompt_gate, wts)

    cache_t = jnp.transpose(cache_nsa_kv, (0, 1, 3, 4, 5, 2))
    wbuf_t = jnp.transpose(state_win_kv, (0, 1, 3, 4, 5, 2))
    n_pages = 8

    def sample_nsa(i, z, gl, cos, sin):
        q, nsa_new, win_new = _rope_sample(z, cos, sin, bs, ss)
        kc, vc = _cmp_means(cache_t, i, page_table, n_pages)
        o_b = _nsa_sample(cache_t, i, page_table, q, kc, vc, nsa_new, win_new, wbuf_t, gl, n_pages, past, ss)
        win_new_t = jnp.transpose(win_new.reshape(bs, ss, 2, KV_HEADS, HEAD_DIM), (0, 2, 3, 4, 1))
        win_t = jnp.concatenate([wbuf_t[i], win_new_t], axis=-1)[..., ss:]
        win_state = jnp.transpose(win_t, (0, 4, 1, 2, 3))
        return o_b, nsa_new.reshape(bs, ss, 4, KV_HEADS, HEAD_DIM), win_state

    def sample_gate(l, which, tm, tn):
        gate = jnp.repeat(mod4[l, which, bp:bp + bs], ss, axis=0)
        return (pl.BlockSpec((bs * ss, tn), lambda j, i: (0, j)), gate)

    (y_s, nsa_s, win_s, s_s, c_s, dv_s, f_s) = _trunk(
        x_sample.reshape(bs * ss, d), mod5, bp, bs, ss, past + jnp.arange(ss, dtype=jnp.int32),
        state_sconv, state_cconv, state_ffn_conv, sample_nsa, ss, bs * ss, sample_gate, wts)

    nsa_p = [a.reshape(bp, sp, 4, KV_HEADS, HEAD_DIM) for a in nsa_p]
    win_p = [a.reshape(bp, -1, 2, KV_HEADS, HEAD_DIM) for a in win_p]
    win_s = [a.reshape(bs, -1, 2, KV_HEADS, HEAD_DIM) for a in win_s]
    dv_s = [a.reshape(bs, ss, -1) for a in dv_s]
    return (y_p.reshape(bp, sp, d), y_s.reshape(bs, ss, d), jnp.stack(nsa_p), jnp.stack(nsa_s),
            jnp.stack(win_p), jnp.stack(win_s), jnp.stack(s_p), jnp.stack(s_s), jnp.stack(c_p),
            jnp.stack(c_s), jnp.stack(dv_s), jnp.stack(f_p), jnp.stack(f_s))
```

```python
import functools
import math

import jax
import jax.numpy as jnp
from jax import lax
from jax.experimental import pallas as pl
from jax.experimental.pallas import tpu as pltpu

BF = jnp.bfloat16
F32 = jnp.float32

HEAD_DIM = 64
N_HEADS = 16
KV_HEADS = 4
GQA = N_HEADS // KV_HEADS
NSA_BLOCK = 64
N_SEL = 16
WINDOW = 512
PAGE_SIZE = 128
ROPE_THETA = 10000.0
CCONV_W = 31
D_CHUNK = 128
D_GROUPS = 4
EPS = 1e-6
NEG = -1e30

LANES = 128
VMEM_LIMIT = 56 * 1024 * 1024


def _params(*sem):
    return pltpu.CompilerParams(dimension_semantics=sem, vmem_limit_bytes=VMEM_LIMIT)


def _sigmoid(x):
    return 1.0 / (1.0 + jnp.exp(-x))


def _silu(x):
    return x * _sigmoid(x)


def _gelu_tanh(x):
    return 0.5 * x * (1.0 + jnp.tanh(math.sqrt(2.0 / math.pi) * (x + 0.044715 * (x * x * x))))


def _layernorm(x, g, b):
    mu = jnp.mean(x, axis=-1, keepdims=True)
    xc = x - mu
    return xc * lax.rsqrt(jnp.mean(xc * xc, axis=-1, keepdims=True) + EPS) * g + b


def _ada_kernel(c_ref, w_ref, b_ref, o_ref):
    ca = _silu(c_ref[...]).astype(BF)
    acc = jnp.dot(ca, w_ref[...].astype(BF), preferred_element_type=F32)
    o_ref[...] = acc + b_ref[...]


def _ada(c16, w_ada, b_ada):
    depth, d, n6 = w_ada.shape
    rows = c16.shape[0]
    tn = 1024
    per = d // tn
    return pl.pallas_call(
        _ada_kernel,
        grid=(depth, n6 // tn),
        in_specs=[
            pl.BlockSpec((rows, d), lambda l, j: (0, 0)),
            pl.BlockSpec((None, d, tn), lambda l, j: (l, 0, j)),
            pl.BlockSpec((None, 1, tn), lambda l, j: (l, 0, j)),
        ],
        out_specs=pl.BlockSpec((None, None, rows, tn), lambda l, j: (l, j // per, 0, j % per)),
        out_shape=jax.ShapeDtypeStruct((depth, 6, rows, d), F32),
        compiler_params=_params("arbitrary", "arbitrary"),
        name="ada",
    )(c16, w_ada, b_ada.reshape(depth, 1, n6))


def _norm_mod_kernel(x_ref, g_ref, sh_ref, sc_ref, o_ref):
    x = x_ref[...]
    y = x * lax.rsqrt(jnp.mean(x * x, axis=-1, keepdims=True) + EPS) * g_ref[...]
    o_ref[...] = (y * (1.0 + sc_ref[...]) + sh_ref[...]).astype(o_ref.dtype)


def _norm_mod(x, g, mod5, l, which, b_off, tm, tpb):
    m, d = x.shape
    mod_spec = lambda w: pl.BlockSpec((None, None, None, 1, d),
                                      lambda i: (l, w, b_off + i // tpb, 0, 0))
    return pl.pallas_call(
        _norm_mod_kernel,
        grid=(m // tm,),
        in_specs=[
            pl.BlockSpec((tm, d), lambda i: (i, 0)),
            pl.BlockSpec((None, 1, d), lambda i: (l, 0, 0)),
            mod_spec(which), mod_spec(which + 1),
        ],
        out_specs=pl.BlockSpec((tm, d), lambda i: (i, 0)),
        out_shape=jax.ShapeDtypeStruct((m, d), BF),
        compiler_params=_params("arbitrary"),
        name="norm_mod",
    )(x, g.reshape(g.shape[0], 1, d), mod5, mod5)


def _final_norm_kernel(x_ref, g_ref, o_ref):
    x = x_ref[...]
    o_ref[...] = x * lax.rsqrt(jnp.mean(x * x, axis=-1, keepdims=True) + EPS) * g_ref[...]


def _final_norm(x, g, tm):
    m, d = x.shape
    return pl.pallas_call(
        _final_norm_kernel,
        grid=(m // tm,),
        in_specs=[pl.BlockSpec((tm, d), lambda i: (i, 0)), pl.BlockSpec((1, d), lambda i: (0, 0))],
        out_specs=pl.BlockSpec((tm, d), lambda i: (i, 0)),
        out_shape=jax.ShapeDtypeStruct((m, d), F32),
        compiler_params=_params("arbitrary"),
        name="final_norm",
    )(x, g.reshape(1, d))


def _mm_kernel(*refs, k_sizes, res_gate, wt, second):
    n_a = len(k_sizes)
    per = n_a + (2 if res_gate else 0)
    w_ref = refs[0]
    groups = [refs[1:1 + per]] + ([refs[1 + per:1 + 2 * per]] if second else [])
    pos = 1 + per * len(groups)
    o_refs = refs[pos:pos + len(groups)]
    wb_ref = refs[pos + len(groups)]

    @pl.when(pl.program_id(1) == 0)
    def _():
        wb_ref[...] = w_ref[...].astype(BF)

    def product(group, o_ref):
        acc = None
        k0 = 0
        for a_ref, ks in zip(group[:n_a], k_sizes):
            if wt:
                part = lax.dot_general(a_ref[...], wb_ref[:, k0:k0 + ks], (((1,), (1,)), ((), ())),
                                       preferred_element_type=F32)
            else:
                part = jnp.dot(a_ref[...], wb_ref[k0:k0 + ks, :], preferred_element_type=F32)
            acc = part if acc is None else acc + part
            k0 += ks
        if res_gate:
            acc = group[n_a][...] + group[n_a + 1][...] * acc
        o_ref[...] = acc.astype(o_ref.dtype)

    product(groups[0], o_refs[0])
    if second:
        @pl.when(pl.program_id(1) == pl.num_programs(1) - 1)
        def _():
            product(groups[1], o_refs[1])


def _mm(a_list, w, l, n, tm, tn, wt=False, res=None, gate_spec=None, gate=None, second=None, name="mm"):
    m = a_list[0].shape[0]
    k_sizes = tuple(a.shape[1] for a in a_list)
    k = sum(k_sizes)
    assert w.shape[2 if wt else 1] == k and n % tn == 0 and m % tm == 0
    if wt:
        in_specs = [pl.BlockSpec((None, tn, k), lambda j, i: (l, j, 0))]
    else:
        in_specs = [pl.BlockSpec((None, k, tn), lambda j, i: (l, 0, j))]
    args = [w]
    in_specs += [pl.BlockSpec((tm, ks), lambda j, i: (i, 0)) for ks in k_sizes]
    args += list(a_list)
    if res is not None:
        in_specs += [pl.BlockSpec((tm, tn), lambda j, i: (i, j)), gate_spec]
        args += [res, gate]
    out_specs = [pl.BlockSpec((tm, tn), lambda j, i: (i, j))]
    out_shape = [jax.ShapeDtypeStruct((m, n), F32)]
    if second is not None:
        a_list2, res2, gate2 = second
        m2 = a_list2[0].shape[0]
        in_specs += [pl.BlockSpec((m2, ks), lambda j, i: (0, 0)) for ks in k_sizes]
        args += list(a_list2)
        if res is not None:
            in_specs += [pl.BlockSpec((m2, tn), lambda j, i: (0, j))] * 2
            args += [res2, gate2]
        out_specs.append(pl.BlockSpec((m2, tn), lambda j, i: (0, j)))
        out_shape.append(jax.ShapeDtypeStruct((m2, n), F32))
    outs = pl.pallas_call(
        functools.partial(_mm_kernel, k_sizes=k_sizes, res_gate=res is not None, wt=wt, second=second is not None),
        grid=(n // tn, m // tm),
        in_specs=in_specs,
        out_specs=out_specs,
        out_shape=out_shape,
        scratch_shapes=[pltpu.VMEM((tn, k) if wt else (k, tn), BF)],
        compiler_params=_params("arbitrary", "arbitrary"),
        name=name,
    )(*args)
    return outs if second is not None else outs[0]


def _conv3(u, p, w):
    row = lax.broadcasted_iota(jnp.int32, u.shape, 0)
    um1 = jnp.where(row == 0, p[1:2], pltpu.roll(u, 1, 0))
    um2 = jnp.where(row == 0, p[0:1], jnp.where(row == 1, p[1:2], pltpu.roll(u, 2, 0)))
    return w[0:1] * um2 + w[1:2] * um1 + w[2:3] * u


def _prev_rows(i, tpb, prev_ref, carry_ref):
    if tpb == 1:
        return prev_ref[...]
    return jnp.where(i % tpb == 0, prev_ref[...], carry_ref[6:8, :])


def _mixer_a_kernel(ain_ref, ab_ref, ac_ref, w_ref, prev_ref, o_ref, st_ref, carry_ref, *, tm, tpb):
    i = pl.program_id(0)
    u = ac_ref[...] * ain_ref[...]
    p = _prev_rows(i, tpb, prev_ref, carry_ref)
    o_ref[...] = (ab_ref[...] * _conv3(u, p, w_ref[...])).astype(o_ref.dtype)
    st_ref[...] = u[tm - 2:tm]
    if tpb > 1:
        carry_ref[...] = u[tm - 8:tm]


def _mixer_a(z, sconv_w, l, prev, tm, tpb):
    m = z.shape[0]
    c = prev.shape[-1]
    nb = prev.shape[0]
    return pl.pallas_call(
        functools.partial(_mixer_a_kernel, tm=tm, tpb=tpb),
        grid=(m // tm,),
        in_specs=[
            pl.BlockSpec((tm, c), lambda i: (i, 0)),
            pl.BlockSpec((tm, c), lambda i: (i, 1)),
            pl.BlockSpec((tm, c), lambda i: (i, 2)),
            pl.BlockSpec((None, 3, c), lambda i: (l, 0, 0)),
            pl.BlockSpec((None, 2, c), lambda i: (i // tpb, 0, 0)),
        ],
        out_specs=[
            pl.BlockSpec((tm, c), lambda i: (i, 0)),
            pl.BlockSpec((None, 2, c), lambda i: (i // tpb, 0, 0)),
        ],
        out_shape=[jax.ShapeDtypeStruct((m, c), BF), jax.ShapeDtypeStruct((nb, 2, c), F32)],
        scratch_shapes=[pltpu.VMEM((8, c), F32)],
        compiler_params=_params("arbitrary"),
        name="mixer_a",
    )(z, z, z, sconv_w, prev)


def _ffn_act_kernel(a_ref, g_ref, w_ref, prev_ref, o_ref, st_ref, carry_ref, *, tm, tpb):
    i = pl.program_id(1)
    a = a_ref[...]
    p = _prev_rows(i, tpb, prev_ref, carry_ref)
    o_ref[...] = (_silu(_conv3(a, p, w_ref[...])) * g_ref[...]).astype(o_ref.dtype)
    st_ref[...] = a[tm - 2:tm]
    if tpb > 1:
        carry_ref[...] = a[tm - 8:tm]


def _ffn_act(za, zg, conv_w, l, prev, tm, tpb, tn):
    m = za.shape[0]
    dff = prev.shape[-1]
    nb = prev.shape[0]
    nj = dff // tn
    return pl.pallas_call(
        functools.partial(_ffn_act_kernel, tm=tm, tpb=tpb),
        grid=(nj, m // tm),
        in_specs=[
            pl.BlockSpec((tm, tn), lambda j, i: (i, j)),
            pl.BlockSpec((tm, tn), lambda j, i: (i, j)),
            pl.BlockSpec((None, 3, tn), lambda j, i: (l, 0, j)),
            pl.BlockSpec((None, 2, tn), lambda j, i: (i // tpb, 0, j)),
        ],
        out_specs=[
            pl.BlockSpec((tm, tn), lambda j, i: (i, j)),
            pl.BlockSpec((None, 2, tn), lambda j, i: (i // tpb, 0, j)),
        ],
        out_shape=[jax.ShapeDtypeStruct((m, dff), BF), jax.ShapeDtypeStruct((nb, 2, dff), F32)],
        scratch_shapes=[pltpu.VMEM((8, tn), F32)],
        compiler_params=_params("arbitrary", "arbitrary"),
        name="ffn_act",
    )(za, zg, conv_w, prev)


def _ffn_up_kernel(h_ref, wa_ref, wg_ref, cw_ref, prev_ref, h2_ref, o_ref, st_ref, a2_ref, g2_ref,
                   wab_ref, wgb_ref, carry_ref, *, tm, tpb):
    i = pl.program_id(1)

    @pl.when(i == 0)
    def _():
        wab_ref[...] = wa_ref[...].astype(BF)
        wgb_ref[...] = wg_ref[...].astype(BF)

    h = h_ref[...]
    a = jnp.dot(h, wab_ref[...], preferred_element_type=F32)
    g = jnp.dot(h, wgb_ref[...], preferred_element_type=F32)
    p = _prev_rows(i, tpb, prev_ref, carry_ref)
    o_ref[...] = (_silu(_conv3(a, p, cw_ref[...])) * g).astype(o_ref.dtype)
    st_ref[...] = a[tm - 2:tm]
    if tpb > 1:
        carry_ref[...] = a[tm - 8:tm]

    @pl.when(i == pl.num_programs(1) - 1)
    def _():
        a2_ref[...] = jnp.dot(h2_ref[...], wab_ref[...], preferred_element_type=F32)
        g2_ref[...] = jnp.dot(h2_ref[...], wgb_ref[...], preferred_element_type=F32)


def _ffn_up(h, w_up, conv_w, l, prev, tm, tpb, tn, h2):
    m, d = h.shape
    m2 = h2.shape[0]
    dff = prev.shape[-1]
    nb = prev.shape[0]
    nj = dff // tn
    small = pl.BlockSpec((m2, tn), lambda j, i: (0, j))
    return pl.pallas_call(
        functools.partial(_ffn_up_kernel, tm=tm, tpb=tpb),
        grid=(nj, m // tm),
        in_specs=[
            pl.BlockSpec((tm, d), lambda j, i: (i, 0)),
            pl.BlockSpec((None, d, tn), lambda j, i: (l, 0, j)),
            pl.BlockSpec((None, d, tn), lambda j, i: (l, 0, j + nj)),
            pl.BlockSpec((None, 3, tn), lambda j, i: (l, 0, j)),
            pl.BlockSpec((None, 2, tn), lambda j, i: (i // tpb, 0, j)),
            pl.BlockSpec((m2, d), lambda j, i: (0, 0)),
        ],
        out_specs=[
            pl.BlockSpec((tm, tn), lambda j, i: (i, j)),
            pl.BlockSpec((None, 2, tn), lambda j, i: (i // tpb, 0, j)),
            small, small,
        ],
        out_shape=[jax.ShapeDtypeStruct((m, dff), BF), jax.ShapeDtypeStruct((nb, 2, dff), F32),
                   jax.ShapeDtypeStruct((m2, dff), F32), jax.ShapeDtypeStruct((m2, dff), F32)],
        scratch_shapes=[pltpu.VMEM((d, tn), BF), pltpu.VMEM((d, tn), BF), pltpu.VMEM((8, tn), F32)],
        compiler_params=_params("arbitrary", "arbitrary"),
        name="ffn_up_fused",
    )(h, w_up, w_up, conv_w, prev, h2)


def _odd_post_kernel(ca_ref, cg_ref, du_ref, dv_ref, cw_ref, cb_ref, clg_ref, clb_ref, dlg_ref,
                     dlb_ref, ws_ref, bst_ref, prev_ref, o_ref, st_ref, v_ref, cbuf, wbuf, *, tm, tpb):
    i = pl.program_id(0)
    cw = cw_ref.shape[-1]
    c = ca_ref[...] * _sigmoid(cg_ref[...])
    if tpb == 1:
        cbuf[2:32, :] = prev_ref[...]
    else:
        @pl.when(i % tpb == 0)
        def _():
            cbuf[2:32, :] = prev_ref[...]

        @pl.when(i % tpb != 0)
        def _():
            cbuf[0:32, :] = cbuf[tm:tm + 32, :]
    cbuf[32:32 + tm, :] = c
    acc = None
    for r in range(8):
        taps = range(r, CCONV_W, 8)
        rows = tm + 8 * (len(taps) - 1)
        wbuf[0:rows, :] = cbuf[2 + r:2 + r + rows, :]
        for j, k in enumerate(taps):
            term = cw_ref[k:k + 1, :] * wbuf[8 * j:8 * j + tm, :]
            acc = term if acc is None else acc + term
    st_ref[...] = cbuf[tm + 2:tm + 32, :]
    o_ref[:, 0:cw] = _silu(_layernorm(acc + cb_ref[...], clg_ref[...], clb_ref[...])).astype(o_ref.dtype)

    u = _gelu_tanh(du_ref[...])
    v = _layernorm(_gelu_tanh(dv_ref[...]), dlg_ref[...], dlb_ref[...])
    v_ref[...] = v
    gw = cw // D_GROUPS
    trow = lax.broadcasted_iota(jnp.int32, (D_CHUNK, D_CHUNK), 0)
    tcol = lax.broadcasted_iota(jnp.int32, (D_CHUNK, D_CHUNK), 1)
    rows = min(tm, D_CHUNK)
    for ch in range(max(1, tm // D_CHUNK)):
        r0 = ch * D_CHUNK
        vch = v[r0:r0 + rows]
        if rows < D_CHUNK:
            vch = jnp.concatenate([vch, jnp.zeros((D_CHUNK - rows, cw), F32)], axis=0)
        vch = vch.astype(BF)
        for g in range(D_GROUPS):
            wg = jnp.where(tcol <= trow, ws_ref[g], 0.0).astype(BF)
            zz = jnp.dot(wg, vch[:, g * gw:(g + 1) * gw], preferred_element_type=F32)
            zz = zz + bst_ref[:, g:g + 1]
            o_ref[r0:r0 + rows, cw + g * gw:cw + (g + 1) * gw] = (
                u[r0:r0 + rows, g * gw:(g + 1) * gw] * zz[0:rows]).astype(o_ref.dtype)


def _odd_post(z, i_odd, prev, cconv_w, cconv_b, c_ln_g, c_ln_b, d_ln_g, d_ln_b, d_ws, d_bs_t, tm, tpb):
    m = z.shape[0]
    c = prev.shape[-1]
    nb = prev.shape[0]
    vec = lambda: pl.BlockSpec((None, 1, c), lambda i: (i_odd, 0, 0))
    r3 = lambda a: a.reshape(a.shape[0], 1, c)
    return pl.pallas_call(
        functools.partial(_odd_post_kernel, tm=tm, tpb=tpb),
        grid=(m // tm,),
        in_specs=[
            pl.BlockSpec((tm, c), lambda i: (i, 0)),
            pl.BlockSpec((tm, c), lambda i: (i, 1)),
            pl.BlockSpec((tm, c), lambda i: (i, 2)),
            pl.BlockSpec((tm, c), lambda i: (i, 3)),
            pl.BlockSpec((None, CCONV_W, c), lambda i: (i_odd, 0, 0)),
            vec(), vec(), vec(), vec(), vec(),
            pl.BlockSpec((None, D_GROUPS, D_CHUNK, D_CHUNK), lambda i: (i_odd, 0, 0, 0)),
            pl.BlockSpec((None, D_CHUNK, D_GROUPS), lambda i: (i_odd, 0, 0)),
            pl.BlockSpec((None, CCONV_W - 1, c), lambda i: (i // tpb, 0, 0)),
        ],
        out_specs=[
            pl.BlockSpec((tm, 2 * c), lambda i: (i, 0)),
            pl.BlockSpec((None, CCONV_W - 1, c), lambda i: (i // tpb, 0, 0)),
            pl.BlockSpec((tm, c), lambda i: (i, 0)),
        ],
        out_shape=[jax.ShapeDtypeStruct((m, 2 * c), BF),
                   jax.ShapeDtypeStruct((nb, CCONV_W - 1, c), F32),
                   jax.ShapeDtypeStruct((m, c), F32)],
        scratch_shapes=[pltpu.VMEM((32 + tm, c), F32), pltpu.VMEM((24 + tm, c), F32)],
        compiler_params=_params("arbitrary"),
        name="odd_post",
    )(z, z, z, z, cconv_w, r3(cconv_b), r3(c_ln_g), r3(c_ln_b), r3(d_ln_g), r3(d_ln_b), d_ws, d_bs_t, prev)


def _rope128(x, cos, sin_signed):
    lane = lax.broadcasted_iota(jnp.int32, x.shape, 1)
    swapped = jnp.where((lane & (HEAD_DIM - 1)) < HEAD_DIM // 2,
                        pltpu.roll(x, LANES - HEAD_DIM // 2, 1), pltpu.roll(x, HEAD_DIM // 2, 1))
    return x * cos + swapped * sin_signed


def _rope_slab(ref, c0, width, cos, sin_signed):
    return jnp.concatenate(
        [_rope128(ref[:, c0 + k * LANES:c0 + (k + 1) * LANES], cos, sin_signed) for k in range(width // LANES)],
        axis=1)


def _rope_prompt_kernel(zq_ref, zn_ref, zw_ref, cos_ref, sin_ref, qt_ref, nsat_ref, wint_ref, kv_ref, kc_ref,
                        vc_ref, *, ts):
    cos, sin = cos_ref[...], sin_ref[...]
    kvw = KV_HEADS * HEAD_DIM
    scale = HEAD_DIM ** -0.5
    for k in range(N_HEADS * HEAD_DIM // LANES):
        rt = (_rope128(zq_ref[:, k * LANES:(k + 1) * LANES], cos, sin) * scale).T
        qt_ref[2 * k] = rt[:HEAD_DIM].astype(qt_ref.dtype)
        qt_ref[2 * k + 1] = rt[HEAD_DIM:].astype(qt_ref.dtype)
    k_cmp = _rope_slab(zn_ref, 0, kvw, cos, sin)
    v_cmp = zn_ref[:, kvw:2 * kvw]
    k_sel = _rope_slab(zn_ref, 2 * kvw, kvw, cos, sin)
    v_sel = zn_ref[:, 3 * kvw:4 * kvw]
    k_win = _rope_slab(zw_ref, 0, kvw, cos, sin)
    v_win = zw_ref[:, kvw:2 * kvw]
    for t, slab in enumerate((k_cmp, v_cmp, k_sel, v_sel)):
        nsat_ref[t] = slab.T.reshape(KV_HEADS, HEAD_DIM, ts)
    for t, slab in enumerate((k_win, v_win)):
        wint_ref[t] = slab.T.reshape(KV_HEADS, HEAD_DIM, ts)
    for t, slab in enumerate((k_sel, v_sel, k_win, v_win)):
        for g in range(KV_HEADS):
            kv_ref[t, g] = slab[:, g * HEAD_DIM:(g + 1) * HEAD_DIM].astype(kv_ref.dtype)
    nblk = ts // NSA_BLOCK
    kc = jnp.sum(k_cmp.reshape(nblk, NSA_BLOCK, kvw), axis=1) * (1.0 / NSA_BLOCK)
    vc = jnp.sum(v_cmp.reshape(nblk, NSA_BLOCK, kvw), axis=1) * (1.0 / NSA_BLOCK)
    for g in range(KV_HEADS):
        kc_ref[g] = kc[:, g * HEAD_DIM:(g + 1) * HEAD_DIM]
        vc_ref[g] = vc[:, g * HEAD_DIM:(g + 1) * HEAD_DIM]


def _rope_prompt(z, cos, sin, batch, seq, ts):
    tpb = seq // ts
    qw = N_HEADS * HEAD_DIM
    nblk = ts // NSA_BLOCK
    return pl.pallas_call(
        functools.partial(_rope_prompt_kernel, ts=ts),
        grid=(batch * tpb,),
        in_specs=[
            pl.BlockSpec((ts, qw), lambda i: (i, 3)),
            pl.BlockSpec((ts, qw), lambda i: (i, 4)),
            pl.BlockSpec((ts, qw // 2), lambda i: (i, 10)),
            pl.BlockSpec((ts, LANES), lambda i: (i % tpb, 0)),
            pl.BlockSpec((ts, LANES), lambda i: (i % tpb, 0)),
        ],
        out_specs=[
            pl.BlockSpec((None, N_HEADS, HEAD_DIM, ts), lambda i: (i // tpb, 0, 0, i % tpb)),
            pl.BlockSpec((None, 4, KV_HEADS, HEAD_DIM, ts), lambda i: (i // tpb, 0, 0, 0, i % tpb)),
            pl.BlockSpec((None, 2, KV_HEADS, HEAD_DIM, ts), lambda i: (i // tpb, 0, 0, 0, i % tpb)),
            pl.BlockSpec((None, 4, KV_HEADS, ts, HEAD_DIM), lambda i: (i // tpb, 0, 0, i % tpb, 0)),
            pl.BlockSpec((None, KV_HEADS, nblk, HEAD_DIM), lambda i: (i // tpb, 0, i % tpb, 0)),
            pl.BlockSpec((None, KV_HEADS, nblk, HEAD_DIM), lambda i: (i // tpb, 0, i % tpb, 0)),
        ],
        out_shape=[
            jax.ShapeDtypeStruct((batch, N_HEADS, HEAD_DIM, seq), BF),
            jax.ShapeDtypeStruct((batch, 4, KV_HEADS, HEAD_DIM, seq), F32),
            jax.ShapeDtypeStruct((batch, 2, KV_HEADS, HEAD_DIM, seq), F32),
            jax.ShapeDtypeStruct((batch, 4, KV_HEADS, seq, HEAD_DIM), BF),
            jax.ShapeDtypeStruct((batch, KV_HEADS, seq // NSA_BLOCK, HEAD_DIM), F32),
            jax.ShapeDtypeStruct((batch, KV_HEADS, seq // NSA_BLOCK, HEAD_DIM), F32),
        ],
        compiler_params=_params("arbitrary"),
        name="rope_prompt",
    )(z, z, z, cos, sin)


def _rope_sample_kernel(zq_ref, zn_ref, zw_ref, cos_ref, sin_ref, q_ref, nsa_ref, win_ref):
    cos, sin = cos_ref[...], sin_ref[...]
    kvw = KV_HEADS * HEAD_DIM
    scale = HEAD_DIM ** -0.5
    for k in range(N_HEADS * HEAD_DIM // LANES):
        r = _rope128(zq_ref[:, k * LANES:(k + 1) * LANES], cos, sin) * scale
        q_ref[2 * k] = r[:, :HEAD_DIM]
        q_ref[2 * k + 1] = r[:, HEAD_DIM:]
    nsa_ref[:, 0:kvw] = _rope_slab(zn_ref, 0, kvw, cos, sin)
    nsa_ref[:, kvw:2 * kvw] = zn_ref[:, kvw:2 * kvw]
    nsa_ref[:, 2 * kvw:3 * kvw] = _rope_slab(zn_ref, 2 * kvw, kvw, cos, sin)
    nsa_ref[:, 3 * kvw:4 * kvw] = zn_ref[:, 3 * kvw:4 * kvw]
    win_ref[:, 0:kvw] = _rope_slab(zw_ref, 0, kvw, cos, sin)
    win_ref[:, kvw:2 * kvw] = zw_ref[:, kvw:2 * kvw]


def _rope_sample(z, cos, sin, batch, seq):
    qw = N_HEADS * HEAD_DIM
    return pl.pallas_call(
        _rope_sample_kernel,
        grid=(batch,),
        in_specs=[
            pl.BlockSpec((seq, qw), lambda i: (i, 3)),
            pl.BlockSpec((seq, qw), lambda i: (i, 4)),
            pl.BlockSpec((seq, qw // 2), lambda i: (i, 10)),
            pl.BlockSpec((seq, LANES), lambda i: (0, 0)),
            pl.BlockSpec((seq, LANES), lambda i: (0, 0)),
        ],
        out_specs=[
            pl.BlockSpec((None, N_HEADS, seq, HEAD_DIM), lambda i: (i, 0, 0, 0)),
            pl.BlockSpec((seq, qw), lambda i: (i, 0)),
            pl.BlockSpec((seq, qw // 2), lambda i: (i, 0)),
        ],
        out_shape=[
            jax.ShapeDtypeStruct((batch, N_HEADS, seq, HEAD_DIM), F32),
            jax.ShapeDtypeStruct((batch * seq, qw), F32),
            jax.ShapeDtypeStruct((batch * seq, qw // 2), F32),
        ],
        compiler_params=_params("arbitrary"),
        name="rope_sample",
    )(z, z, z, cos, sin)


def _cmp_branch(kc, vc, q, qpos, nq):
    nb = kc.shape[0]
    st = lax.dot_general(kc.astype(BF), q, (((1,), (1,)), ((), ())), preferred_element_type=F32)
    blk = lax.broadcasted_iota(jnp.int32, st.shape, 0)
    ok = (blk + 1) * NSA_BLOCK <= qpos + 1
    sm = jnp.where(ok, st, NEG)
    mx = jnp.max(sm, axis=0, keepdims=True)
    e = jnp.where(ok, jnp.exp(sm - mx), 0.0)
    den = jnp.sum(e, axis=0, keepdims=True)
    pt = e / jnp.where(den > 0.0, den, 1.0)
    o_cmp = lax.dot_general(pt.astype(BF), vc.astype(BF), (((0,), (0,)), ((), ())), preferred_element_type=F32)
    imp = pt[:, 0:nq]
    for r in range(1, GQA):
        imp = imp + pt[:, r * nq:(r + 1) * nq]
    return o_cmp, imp


def _importance(imp, qpos_q):
    blk = lax.broadcasted_iota(jnp.int32, imp.shape, 0)
    cur = qpos_q // NSA_BLOCK
    forced = (blk == 0) | (blk == cur) | (blk == cur - 1)
    imp = jnp.where(forced, GQA + 1.0, imp)
    return jnp.where(blk <= cur, imp, -1.0)


def _select_topk(imp_ref, nb):
    imp = imp_ref[...]
    blk = lax.broadcasted_iota(jnp.int32, imp.shape, 0)

    def body(i, rank):
        row = imp_ref[pl.ds(i, 1), :]
        ahead = (row > imp) | ((row == imp) & (i < blk))
        return rank + jnp.where(ahead, 1.0, 0.0)

    rank = lax.fori_loop(0, nb, body, jnp.zeros(imp.shape, F32))
    return jnp.where(rank < float(N_SEL), 1.0, 0.0)


def _expand_blocks(sel_t, n_keys, first_block):
    nbl = sel_t.shape[0]
    kb = lax.broadcasted_iota(jnp.int32, (nbl, n_keys), 1) // NSA_BLOCK + first_block
    nn = lax.broadcasted_iota(jnp.int32, (nbl, n_keys), 0)
    e = jnp.where(kb == nn, 1.0, 0.0).astype(BF)
    return lax.dot_general(sel_t.astype(BF), e, (((0,), (0,)), ((), ())), preferred_element_type=F32)


def _online_update(s, valid, vt, m_ref, l_ref, acc_ref, idx, v_rows=None):
    nk = s.shape[1]
    s = jnp.where(valid, s, NEG)
    m_prev = m_ref[idx]
    m_new = jnp.maximum(m_prev, jnp.max(s, axis=1, keepdims=True))
    alpha = jnp.exp(m_prev - m_new)
    p = jnp.where(valid, jnp.exp(s - jnp.concatenate([m_new] * (nk // LANES), axis=1)), 0.0)
    l_ref[idx] = alpha * l_ref[idx] + jnp.sum(p, axis=1, keepdims=True)
    if v_rows is None:
        pv = lax.dot_general(p.astype(BF), vt, (((1,), (1,)), ((), ())), preferred_element_type=F32)
    else:
        pv = jnp.dot(p.astype(BF), v_rows, preferred_element_type=F32)
    acc_ref[idx] = alpha[:, :HEAD_DIM] * acc_ref[idx] + pv
    m_ref[idx] = m_new


def _online_update_t(k_rows, qt, bias, v_rows, m_ref, l_ref, acc_ref, idx):
    st = jnp.dot(k_rows, qt, preferred_element_type=F32)
    if bias is not None:
        nq = bias.shape[1]
        st = jnp.concatenate([st[:, r * nq:(r + 1) * nq] + bias for r in range(GQA)], axis=1)
    m_prev = m_ref[idx]
    m_new = jnp.maximum(m_prev, jnp.max(st, axis=0, keepdims=True))
    alpha = jnp.exp(m_prev - m_new)
    p = jnp.exp(st - m_new)
    l_ref[idx] = alpha * l_ref[idx] + jnp.sum(p, axis=0, keepdims=True)
    pv = lax.dot_general(v_rows, p.astype(BF), (((0,), (0,)), ((), ())), preferred_element_type=F32)
    acc_ref[idx] = alpha * acc_ref[idx] + pv
    m_ref[idx] = m_new


def _nsa_prompt_kernel(qt_ref, kc_ref, vc_ref, ks_ref, vs_ref, kw_ref, vw_ref, gl_ref, o_ref,
                       imp_ref, m_ref, l_ref, acc_ref, *, tq, seq):
    qi = pl.program_id(2)
    q0 = qi * tq
    rq = GQA * tq
    nb = seq // NSA_BLOCK
    bpt = tq // NSA_BLOCK
    qt = jnp.concatenate([qt_ref[r] for r in range(GQA)], axis=1)

    st = jnp.dot(kc_ref[...].astype(BF), qt, preferred_element_type=F32)
    blk = lax.broadcasted_iota(jnp.int32, (nb, rq), 0)
    qpos_r = q0 + (lax.broadcasted_iota(jnp.int32, (nb, rq), 1) & (tq - 1))
    ok = (blk + 1) * NSA_BLOCK <= qpos_r + 1
    sm = jnp.where(ok, st, NEG)
    e = jnp.where(ok, jnp.exp(sm - jnp.max(sm, axis=0, keepdims=True)), 0.0)
    den = jnp.sum(e, axis=0, keepdims=True)
    pt = e / jnp.where(den > 0.0, den, 1.0)
    o_cmp = lax.dot_general(vc_ref[...].astype(BF), pt.astype(BF), (((0,), (0,)), ((), ())),
                            preferred_element_type=F32)
    imp = pt[:, 0:tq]
    for r in range(1, GQA):
        imp = imp + pt[:, r * tq:(r + 1) * tq]
    imp_ref[...] = _importance(imp, q0 + lax.broadcasted_iota(jnp.int32, (nb, tq), 1))
    sel_bf = _select_topk(imp_ref, nb).astype(BF)

    m_ref[...] = jnp.full(m_ref.shape, NEG, F32)
    l_ref[...] = jnp.zeros(l_ref.shape, F32)
    acc_ref[...] = jnp.zeros(acc_ref.shape, F32)

    krow = lax.broadcasted_iota(jnp.int32, (tq, tq), 0)
    qcol = lax.broadcasted_iota(jnp.int32, (tq, tq), 1)
    e_rows = lax.broadcasted_iota(jnp.int32, (tq, nb), 0) // NSA_BLOCK
    e_cols = lax.broadcasted_iota(jnp.int32, (tq, nb), 1)

    def sel_bias(c):
        e = jnp.where(e_rows + c * bpt == e_cols, 1.0, 0.0).astype(BF)
        return (jnp.dot(e, sel_bf, preferred_element_type=F32) - 1.0) * (-NEG)

    def rows_of(c):
        return pl.ds(pl.multiple_of(c * tq, tq), tq)

    def sel_chunk(c):
        _online_update_t(ks_ref[rows_of(c), :], qt, sel_bias(c), vs_ref[rows_of(c), :], m_ref, l_ref, acc_ref, 0)

    def sel_pair(j, carry):
        sel_chunk(2 * j)
        sel_chunk(2 * j + 1)
        return carry

    n_full = (WINDOW - tq) // tq
    n_far = jnp.maximum(qi - (n_full + 1), 0)
    lax.fori_loop(0, n_far // 2, sel_pair, 0)

    @pl.when(n_far % 2 == 1)
    def _():
        sel_chunk(n_far - 1)
    for rel in range(n_full + 1, 0, -1):
        @pl.when(qi >= rel)
        def _(rel=rel):
            c = qi - rel
            sel_chunk(c)
            bias = None if rel <= n_full else jnp.where(krow > qcol + (rel * tq - WINDOW), 0.0, NEG)
            _online_update_t(kw_ref[rows_of(c), :], qt, bias, vw_ref[rows_of(c), :], m_ref, l_ref, acc_ref, 1)
    causal = krow <= qcol
    _online_update_t(ks_ref[rows_of(qi), :], qt, jnp.where(causal, sel_bias(qi), NEG), vs_ref[rows_of(qi), :],
                     m_ref, l_ref, acc_ref, 0)
    _online_update_t(kw_ref[rows_of(qi), :], qt, jnp.where(causal, 0.0, NEG), vw_ref[rows_of(qi), :],
                     m_ref, l_ref, acc_ref, 1)

    o_sel = acc_ref[0] * (1.0 / l_ref[0])
    o_win = acc_ref[1] * (1.0 / l_ref[1])
    gate_t = _sigmoid(gl_ref[...]).T
    outs = []
    for r in range(GQA):
        cols = slice(r * tq, (r + 1) * tq)
        o_t = (gate_t[3 * r:3 * r + 1] * o_cmp[:, cols] + gate_t[3 * r + 1:3 * r + 2] * o_sel[:, cols]
               + gate_t[3 * r + 2:3 * r + 3] * o_win[:, cols])
        outs.append(o_t.T)
    o_ref[...] = jnp.concatenate(outs, axis=1).astype(o_ref.dtype)


def _nsa_prompt(qt, kc, vc, kv, gl, batch, seq, tq):
    nq = seq // tq
    nb = seq // NSA_BLOCK
    rq = GQA * tq
    assert WINDOW % tq == 0
    kv_spec = lambda t: pl.BlockSpec((None, None, None, seq, HEAD_DIM), lambda b, g, i: (b, t, g, 0, 0))
    cmp_spec = pl.BlockSpec((None, None, nb, HEAD_DIM), lambda b, g, i: (b, g, 0, 0))
    return pl.pallas_call(
        functools.partial(_nsa_prompt_kernel, tq=tq, seq=seq),
        grid=(batch, KV_HEADS, nq),
        in_specs=[
            pl.BlockSpec((None, GQA, HEAD_DIM, tq), lambda b, g, i: (b, g, 0, i)),
            cmp_spec, cmp_spec,
            kv_spec(0), kv_spec(1), kv_spec(2), kv_spec(3),
            pl.BlockSpec((tq, LANES), lambda b, g, i: (b * nq + i, g)),
        ],
        out_specs=pl.BlockSpec((tq, GQA * HEAD_DIM), lambda b, g, i: (b * nq + i, g)),
        out_shape=jax.ShapeDtypeStruct((batch * seq, N_HEADS * HEAD_DIM), BF),
        scratch_shapes=[
            pltpu.VMEM((nb, tq), F32),
            pltpu.VMEM((2, 1, rq), F32),
            pltpu.VMEM((2, 1, rq), F32),
            pltpu.VMEM((2, HEAD_DIM, rq), F32),
        ],
        compiler_params=_params("arbitrary", "arbitrary", "arbitrary"),
        name="nsa_prompt",
    )(qt, kc, vc, kv, kv, kv, kv, gl)


def _cmp_means_kernel(pt_ref, *refs, n_pages):
    page_refs = refs[:n_pages]
    kc_ref, vc_ref = refs[n_pages], refs[n_pages + 1]
    bpp = PAGE_SIZE // NSA_BLOCK
    kvw = KV_HEADS * HEAD_DIM
    for t, out in enumerate((kc_ref, vc_ref)):
        x = jnp.concatenate([page_refs[p][t].reshape(kvw, PAGE_SIZE).T for p in range(n_pages)], axis=0)
        out[...] = jnp.sum(x.reshape(n_pages * bpp, NSA_BLOCK, kvw), axis=1) * (1.0 / NSA_BLOCK)


def _cmp_means(cache_t, layer, page_table, n_pages):
    batch, ppb = page_table.shape
    kvw = KV_HEADS * HEAD_DIM
    bpp = PAGE_SIZE // NSA_BLOCK
    steps = ppb // n_pages

    def page_spec(p):
        return pl.BlockSpec((None, None, 2, KV_HEADS, HEAD_DIM, PAGE_SIZE),
                            lambda b, s, pt: (layer, pt[b, s * n_pages + p], 0, 0, 0, 0))

    out_spec = pl.BlockSpec((None, n_pages * bpp, kvw), lambda b, s, pt: (b, s, 0))
    return pl.pallas_call(
        functools.partial(_cmp_means_kernel, n_pages=n_pages),
        grid_spec=pltpu.PrefetchScalarGridSpec(
            num_scalar_prefetch=1, grid=(batch, steps),
            in_specs=[page_spec(p) for p in range(n_pages)],
            out_specs=[out_spec, out_spec]),
        out_shape=[jax.ShapeDtypeStruct((batch, ppb * bpp, kvw), F32)] * 2,
        compiler_params=_params("arbitrary", "arbitrary"),
        name="cmp_means",
    )(page_table, *([cache_t] * n_pages))


def _nsa_sample_kernel(pt_ref, *refs, n_pages, past, s_new):
    (q_ref, kc_ref, vc_ref, new_ref, wnew_ref, wbuf_ref, gl_ref) = refs[:7]
    page_refs = refs[7:7 + n_pages]
    o_ref = refs[7 + n_pages]
    kcf, vcf, imp_ref, sel_ref, m_ref, l_ref, acc_ref, ocmp_ref = refs[8 + n_pages:]
    step = pl.program_id(1)
    nsteps = pl.num_programs(1)
    kvw = KV_HEADS * HEAD_DIM
    rq = GQA * s_new
    nbp = past // NSA_BLOCK
    nbf = kcf.shape[0]
    wlen = wbuf_ref.shape[-1]
    pad_rows = LANES - s_new

    def q_of(g):
        return q_ref[g * GQA:(g + 1) * GQA].reshape(rq, HEAD_DIM).astype(BF)

    def pad_keys(x):
        return jnp.concatenate([x, jnp.zeros((pad_rows, HEAD_DIM), F32)], axis=0).astype(BF)

    @pl.when(step == 0)
    def _():
        row8 = lax.broadcasted_iota(jnp.int32, (nbf - nbp, kvw), 0)
        for full, src, c0 in ((kcf, kc_ref, 0), (vcf, vc_ref, kvw)):
            full[0:nbp, :] = src[...]
            mean_new = jnp.sum(new_ref[:, c0:c0 + kvw], axis=0, keepdims=True) * (1.0 / NSA_BLOCK)
            full[nbp:nbf, :] = jnp.where(row8 == 0, mean_new, 0.0)
        col = lax.broadcasted_iota(jnp.int32, (nbf, rq), 1)
        qpos_q = past + lax.broadcasted_iota(jnp.int32, (nbf, s_new), 1)
        for g in range(KV_HEADS):
            lanes = slice(g * HEAD_DIM, (g + 1) * HEAD_DIM)
            o_cmp, imp = _cmp_branch(kcf[:, lanes], vcf[:, lanes], q_of(g), past + (col & (s_new - 1)), s_new)
            ocmp_ref[g] = o_cmp
            imp_ref[:, g * s_new:(g + 1) * s_new] = _importance(imp, qpos_q)
        sel = _select_topk(imp_ref, nbf)
        for g in range(KV_HEADS):
            sel_ref[g] = jnp.concatenate([sel[:, g * s_new:(g + 1) * s_new]] * GQA, axis=1)
        m_ref[...] = jnp.full(m_ref.shape, NEG, F32)
        l_ref[...] = jnp.zeros(l_ref.shape, F32)
        acc_ref[...] = jnp.zeros(acc_ref.shape, F32)

    nk = n_pages * PAGE_SIZE
    nbl = nk // NSA_BLOCK
    for g in range(KV_HEADS):
        kt = jnp.concatenate([page_refs[p][0, g] for p in range(n_pages)], axis=1).astype(BF)
        vt = jnp.concatenate([page_refs[p][1, g] for p in range(n_pages)], axis=1).astype(BF)
        sel_rows = sel_ref[g, pl.ds(pl.multiple_of(step * nbl, nbl), nbl), :]
        valid = _expand_blocks(sel_rows, nk, 0) > 0.5
        s = jnp.dot(q_of(g), kt, preferred_element_type=F32)
        _online_update(s, valid, vt, m_ref, l_ref, acc_ref, g)

    @pl.when(step == nsteps - 1)
    def _():
        tq_col = lax.broadcasted_iota(jnp.int32, (rq, LANES), 0) & (s_new - 1)
        tk = lax.broadcasted_iota(jnp.int32, (rq, LANES), 1)
        new_ok = (tk <= tq_col) & (tk < s_new)
        gate = _sigmoid(gl_ref[...])
        tail = nbf - 16
        for g in range(KV_HEADS):
            q = q_of(g)
            lane0 = 2 * kvw + g * HEAD_DIM
            k_new = pad_keys(new_ref[:, lane0:lane0 + HEAD_DIM])
            v_new = pad_keys(new_ref[:, lane0 + kvw:lane0 + kvw + HEAD_DIM])
            kb = lax.broadcasted_iota(jnp.int32, (16, LANES), 0)
            e = jnp.where(kb == nbp - tail, 1.0, 0.0).astype(BF)
            selx = lax.dot_general(sel_ref[g, tail:nbf, :].astype(BF), e, (((0,), (0,)), ((), ())),
                                   preferred_element_type=F32)
            s = lax.dot_general(q, k_new, (((1,), (1,)), ((), ())), preferred_element_type=F32)
            _online_update(s, (selx > 0.5) & new_ok, None, m_ref, l_ref, acc_ref, g, v_rows=v_new)
            o_sel = acc_ref[g] / l_ref[g][:, :HEAD_DIM]
            wl0 = g * HEAD_DIM
            kw_new = pad_keys(wnew_ref[:, wl0:wl0 + HEAD_DIM])
            vw_new = pad_keys(wnew_ref[:, kvw + wl0:kvw + wl0 + HEAD_DIM])
            sb = jnp.dot(q, wbuf_ref[0, g].astype(BF), preferred_element_type=F32)
            sn = lax.dot_general(q, kw_new, (((1,), (1,)), ((), ())), preferred_element_type=F32)
            jb = lax.broadcasted_iota(jnp.int32, (rq, wlen), 1)
            tq_b = lax.broadcasted_iota(jnp.int32, (rq, wlen), 0) & (s_new - 1)
            ok_b = (past - wlen + jb > past + tq_b - WINDOW) & (past - wlen + jb >= 0)
            sb = jnp.where(ok_b, sb, NEG)
            sn = jnp.where(new_ok, sn, NEG)
            mx = jnp.maximum(jnp.max(sb, axis=1, keepdims=True), jnp.max(sn, axis=1, keepdims=True))
            pb = jnp.where(ok_b, jnp.exp(sb - mx), 0.0)
            pn = jnp.where(new_ok, jnp.exp(sn - mx), 0.0)
            den = jnp.sum(pb, axis=1, keepdims=True) + jnp.sum(pn, axis=1, keepdims=True)
            o_win = (lax.dot_general(pb.astype(BF), wbuf_ref[1, g].astype(BF), (((1,), (1,)), ((), ())),
                                     preferred_element_type=F32)
                     + jnp.dot(pn.astype(BF), vw_new, preferred_element_type=F32)) / den
            o_cmp = ocmp_ref[g]
            for r in range(GQA):
                rows = slice(r * s_new, (r + 1) * s_new)
                c = g * LANES + 3 * r
                h = g * GQA + r
                o_ref[:, h * HEAD_DIM:(h + 1) * HEAD_DIM] = (
                    gate[:, c:c + 1] * o_cmp[rows] + gate[:, c + 1:c + 2] * o_sel[rows]
                    + gate[:, c + 2:c + 3] * o_win[rows]).astype(o_ref.dtype)


def _nsa_sample(cache_t, layer, page_table, q, kc, vc, nsa_new, win_new, wbuf_t, gl, n_pages, past, s_new):
    batch, ppb = page_table.shape
    kvw = KV_HEADS * HEAD_DIM
    qw = N_HEADS * HEAD_DIM
    nbp = past // NSA_BLOCK
    nbf = nbp + 8
    wlen = wbuf_t.shape[-1]
    rq = GQA * s_new
    steps = ppb // n_pages

    def page_spec(p):
        return pl.BlockSpec((None, None, 2, KV_HEADS, HEAD_DIM, PAGE_SIZE),
                            lambda b, s, pt: (layer, pt[b, s * n_pages + p], 1, 0, 0, 0))

    per_b = lambda shape: pl.BlockSpec((None,) + shape, lambda b, s, pt: (b,) + (0,) * len(shape))
    rows_b = lambda w: pl.BlockSpec((s_new, w), lambda b, s, pt: (b, 0))
    return pl.pallas_call(
        functools.partial(_nsa_sample_kernel, n_pages=n_pages, past=past, s_new=s_new),
        grid_spec=pltpu.PrefetchScalarGridSpec(
            num_scalar_prefetch=1, grid=(batch, steps),
            in_specs=[
                per_b((N_HEADS, s_new, HEAD_DIM)),
                per_b((nbp, kvw)), per_b((nbp, kvw)),
                rows_b(4 * kvw), rows_b(2 * kvw),
                pl.BlockSpec((None, None, 2, KV_HEADS, HEAD_DIM, wlen), lambda b, s, pt: (layer, b, 0, 0, 0, 0)),
                rows_b(KV_HEADS * LANES),
            ] + [page_spec(p) for p in range(n_pages)],
            out_specs=rows_b(qw),
            scratch_shapes=[
                pltpu.VMEM((nbf, kvw), F32), pltpu.VMEM((nbf, kvw), F32),
                pltpu.VMEM((nbf, KV_HEADS * s_new), F32),
                pltpu.VMEM((KV_HEADS, nbf, rq), F32),
                pltpu.VMEM((KV_HEADS, rq, LANES), F32),
                pltpu.VMEM((KV_HEADS, rq, LANES), F32),
                pltpu.VMEM((KV_HEADS, rq, HEAD_DIM), F32),
                pltpu.VMEM((KV_HEADS, rq, HEAD_DIM), F32),
            ]),
        out_shape=jax.ShapeDtypeStruct((batch * s_new, qw), BF),
        compiler_params=_params("arbitrary", "arbitrary"),
        name="nsa_sample",
    )(page_table, q, kc, vc, nsa_new, win_new, wbuf_t, gl, *([cache_t] * n_pages))


def _rope_tables(pos):
    half = HEAD_DIM // 2
    freq = ROPE_THETA ** (-jnp.arange(half, dtype=F32) / half)
    ang = pos.astype(F32)[:, None] * freq[None, :]
    cos, sin = jnp.cos(ang), jnp.sin(ang)
    return jnp.tile(cos, (1, LANES // half)), jnp.tile(jnp.concatenate([-sin, sin], axis=1), (1, LANES // HEAD_DIM))


def _trunk(groups, mod4, mod5, wts):
    (g_mix, g_ffn, g_final, w_in_even_t, w_gate_t, sconv_w, w_out_even, w_in_odd, cconv_w, cconv_b, c_ln_g,
     c_ln_b, d_ln_g, d_ln_b, d_ws, d_bs_t, w_out_odd, w_up, ffn_conv_w, w_down) = wts
    prompt, sample = groups
    depth = g_mix.shape[0]
    d = prompt["x"].shape[1]
    sp = prompt["seq"]
    a_w = sconv_w.shape[-1]
    dff = ffn_conv_w.shape[-1]
    kv_cols = 3 * a_w + N_HEADS * HEAD_DIM + 6 * KV_HEADS * HEAD_DIM
    mm_tm = 1024
    down_tm = 512
    out_tn = 1024

    def prompt_gate(l, which, tm, tn):
        return pl.BlockSpec((None, None, None, 1, tn), lambda j, i: (l, which, (i * tm) // sp, 0, j))

    def sample_gate(l, which):
        b0 = sample["b_off"]
        return jnp.repeat(mod4[l, which, b0:b0 + sample["batch"]], sample["seq"], axis=0)

    def norm(xs, g, l, which):
        return [_norm_mod(x, g, mod5, l, which, grp["b_off"], grp["tm"], grp["tpb"]) for x, grp in zip(xs, groups)]

    xs = [prompt["x"], sample["x"]]
    new = [dict(nsa=[], win=[], s=[], c=[], dv=[], f=[]) for _ in groups]
    for l in range(depth):
        i = l // 2
        h = norm(xs, g_mix, l, 0)
        if l % 2 == 0:
            zs = _mm([h[0]], w_in_even_t, i, kv_cols, mm_tm, kv_cols // 4, wt=True, second=([h[1]], None, None),
                     name="in_even")
            gls = _mm([h[0]], w_gate_t, i, KV_HEADS * LANES, mm_tm, 512, wt=True, second=([h[1]], None, None),
                      name="gate_logits")
            mixed = []
            for z, gl, grp, out in zip(zs, gls, groups, new):
                mix_a, sb = _mixer_a(z, sconv_w, i, grp["sconv"][i], grp["tm"], grp["tpb"])
                o_b, nsa_rows, win_state = grp["nsa"](i, z, gl)
                mixed.append([mix_a, o_b])
                out["s"].append(sb)
                out["nsa"].append(nsa_rows)
                out["win"].append(win_state)
            w_out = w_out_even
        else:
            zs = _mm([h[0]], w_in_odd, i, w_in_odd.shape[-1], mm_tm, out_tn, second=([h[1]], None, None),
                     name="in_odd")
            mixed = []
            for z, grp, out in zip(zs, groups, new):
                mix, cb, v = _odd_post(z, i, grp["cconv"][i], cconv_w, cconv_b, c_ln_g, c_ln_b, d_ln_g, d_ln_b,
                                       d_ws, d_bs_t, grp["tm"], grp["tpb"])
                mixed.append([mix])
                out["c"].append(cb)
                out["dv"].append(v)
            w_out = w_out_odd
        xs = list(_mm(mixed[0], w_out, i, d, mm_tm, out_tn, res=xs[0], gate_spec=prompt_gate(l, 2, mm_tm, out_tn),
                      gate=mod5, second=(mixed[1], xs[1], sample_gate(l, 2)), name="out_proj"))
        h = norm(xs, g_ffn, l, 3)
        act_p, fb_p, a_s, g_s = _ffn_up(h[0], w_up, ffn_conv_w, l, prompt["ffn"][l], mm_tm, sp // mm_tm, 512, h[1])
        act_s, fb_s = _ffn_act(a_s, g_s, ffn_conv_w, l, sample["ffn"][l], sample["tm"], sample["tpb"], dff)
        new[0]["f"].append(fb_p)
        new[1]["f"].append(fb_s)
        xs = list(_mm([act_p], w_down, l, d, down_tm, 512, res=xs[0], gate_spec=prompt_gate(l, 5, down_tm, 512),
                      gate=mod5, second=([act_s], xs[1], sample_gate(l, 5)), name="ffn_down"))
    ys = [_final_norm(x, g_final, grp["tm"]) for x, grp in zip(xs, groups)]
    return ys, new


def kernel(x_prompt, x_sample, cache_nsa_kv, state_win_kv, state_sconv, state_cconv, state_ffn_conv, page_table, c_prompt, c_sample, g_mix, g_ffn, g_final, w_ada, b_ada, w_in_even, sconv_w, w_out_even, w_in_odd, cconv_w, cconv_b, c_ln_g, c_ln_b, d_ln_g, d_ln_b, d_ws, d_bs, w_out_odd, w_up, ffn_conv_w, w_down):
    bp, sp, d = x_prompt.shape
    bs, ss, _ = x_sample.shape
    depth = g_mix.shape[0]
    n_even = w_in_even.shape[0]
    past = page_table.shape[1] * PAGE_SIZE
    a_w = sconv_w.shape[-1]
    dff = ffn_conv_w.shape[-1]
    kvw = KV_HEADS * HEAD_DIM
    dt = x_prompt.dtype

    rows = -(-(bp + bs) // 8) * 8
    c_all = jnp.concatenate([c_prompt, c_sample, jnp.zeros((rows - bp - bs, d), dt)], axis=0)
    mod4 = _ada(c_all, w_ada, b_ada)
    mod5 = mod4.reshape(depth, 6, rows, 1, d)

    gate_c0 = 3 * a_w + N_HEADS * HEAD_DIM + 6 * kvw
    w_in_even_t = jnp.swapaxes(w_in_even, 1, 2)
    wg = w_in_even_t[:, gate_c0:, :].reshape(n_even, KV_HEADS, GQA * 3, d)
    w_gate_t = jnp.pad(wg, ((0, 0), (0, 0), (0, LANES - GQA * 3), (0, 0))).reshape(n_even, KV_HEADS * LANES, d)

    wts = (g_mix, g_ffn, g_final, w_in_even_t, w_gate_t, sconv_w, w_out_even, w_in_odd, cconv_w, cconv_b, c_ln_g,
           c_ln_b, d_ln_g, d_ln_b, d_ws, jnp.swapaxes(d_bs, 1, 2), w_out_odd, w_up, ffn_conv_w, w_down)

    tm_p = 512
    tq = 256
    cos_p, sin_p = _rope_tables(jnp.arange(sp, dtype=jnp.int32))

    def prompt_nsa(i, z, gl):
        qt, nsat, wint, kv, kc, vc = _rope_prompt(z, cos_p, sin_p, bp, sp, tm_p)
        o_b = _nsa_prompt(qt, kc, vc, kv, gl, bp, sp, tq)
        keep = min(WINDOW, sp)
        nsa_rows = jnp.transpose(nsat, (0, 4, 1, 2, 3))
        win_state = jnp.transpose(wint[..., sp - keep:], (0, 4, 1, 2, 3))
        return o_b, nsa_rows, win_state

    zeros = lambda n, r, w: jnp.zeros((n, bp, r, w), dt)
    prompt = dict(x=x_prompt.reshape(bp * sp, d), b_off=0, batch=bp, seq=sp, tm=tm_p, tpb=sp // tm_p,
                  sconv=zeros(n_even, 2, a_w), cconv=zeros(depth // 2, CCONV_W - 1, a_w), ffn=zeros(depth, 2, dff),
                  nsa=prompt_nsa)

    cache_t = jnp.transpose(cache_nsa_kv, (0, 1, 3, 4, 5, 2))
    wbuf_t = jnp.transpose(state_win_kv, (0, 1, 3, 4, 5, 2))
    n_pages = 8
    cos_s, sin_s = _rope_tables(past + jnp.arange(ss, dtype=jnp.int32))

    def sample_nsa(i, z, gl):
        q, nsa_new, win_new = _rope_sample(z, cos_s, sin_s, bs, ss)
        kc, vc = _cmp_means(cache_t, i, page_table, n_pages)
        o_b = _nsa_sample(cache_t, i, page_table, q, kc, vc, nsa_new, win_new, wbuf_t, gl, 4 * n_pages, past, ss)
        win_new_t = jnp.transpose(win_new.reshape(bs, ss, 2, KV_HEADS, HEAD_DIM), (0, 2, 3, 4, 1))
        win_t = jnp.concatenate([wbuf_t[i], win_new_t], axis=-1)[..., ss:]
        win_state = jnp.transpose(win_t, (0, 4, 1, 2, 3))
        return o_b, nsa_new.reshape(bs, ss, 4, KV_HEADS, HEAD_DIM), win_state

    sample = dict(x=x_sample.reshape(bs * ss, d), b_off=bp, batch=bs, seq=ss, tm=ss, tpb=1,
                  sconv=state_sconv, cconv=state_cconv, ffn=state_ffn_conv, nsa=sample_nsa)

    (y_p, y_s), (new_p, new_s) = _trunk((prompt, sample), mod4, mod5, wts)

    nsa_p = [a.reshape(bp, sp, 4, KV_HEADS, HEAD_DIM) for a in new_p["nsa"]]
    win_p = [a.reshape(bp, -1, 2, KV_HEADS, HEAD_DIM) for a in new_p["win"]]
    dv_s = [a.reshape(bs, ss, -1) for a in new_s["dv"]]
    return (y_p.reshape(bp, sp, d), y_s.reshape(bs, ss, d), jnp.stack(nsa_p), jnp.stack(new_s["nsa"]),
            jnp.stack(win_p), jnp.stack(new_s["win"]), jnp.stack(new_p["s"]), jnp.stack(new_s["s"]),
            jnp.stack(new_p["c"]), jnp.stack(new_s["c"]), jnp.stack(dv_s), jnp.stack(new_p["f"]),
            jnp.stack(new_s["f"]))
```

```python
import functools
import math

import jax
import jax.numpy as jnp
from jax import lax
from jax.experimental import pallas as pl
from jax.experimental.pallas import tpu as pltpu

BF = jnp.bfloat16
F32 = jnp.float32

HEAD_DIM = 64
N_HEADS = 16
KV_HEADS = 4
GQA = N_HEADS // KV_HEADS
NSA_BLOCK = 64
N_SEL = 16
WINDOW = 512
PAGE_SIZE = 128
ROPE_THETA = 10000.0
CCONV_W = 31
D_CHUNK = 128
D_GROUPS = 4
EPS = 1e-6
NEG = -1e30

Q_SCALE = HEAD_DIM ** -0.5 * math.log2(math.e)

LANES = 128
VMEM_LIMIT = 56 * 1024 * 1024


def _params(*sem):
    return pltpu.CompilerParams(dimension_semantics=sem, vmem_limit_bytes=VMEM_LIMIT)


def _sigmoid(x):
    return 1.0 / (1.0 + jnp.exp(-x))


def _silu(x):
    return x * _sigmoid(x)


def _gelu_tanh(x):
    return 0.5 * x * (1.0 + jnp.tanh(math.sqrt(2.0 / math.pi) * (x + 0.044715 * (x * x * x))))


def _layernorm(x, g, b):
    mu = jnp.mean(x, axis=-1, keepdims=True)
    xc = x - mu
    return xc * lax.rsqrt(jnp.mean(xc * xc, axis=-1, keepdims=True) + EPS) * g + b


def _ada_kernel(c_ref, w_ref, b_ref, o_ref):
    ca = _silu(c_ref[...]).astype(BF)
    acc = jnp.dot(ca, w_ref[...].astype(BF), preferred_element_type=F32)
    o_ref[...] = acc + b_ref[...]


def _ada(c16, w_ada, b_ada):
    depth, d, n6 = w_ada.shape
    rows = c16.shape[0]
    tn = 1024
    per = d // tn
    return pl.pallas_call(
        _ada_kernel,
        grid=(depth, n6 // tn),
        in_specs=[
            pl.BlockSpec((rows, d), lambda l, j: (0, 0)),
            pl.BlockSpec((None, d, tn), lambda l, j: (l, 0, j)),
            pl.BlockSpec((None, 1, tn), lambda l, j: (l, 0, j)),
        ],
        out_specs=pl.BlockSpec((None, None, rows, tn), lambda l, j: (l, j // per, 0, j % per)),
        out_shape=jax.ShapeDtypeStruct((depth, 6, rows, d), F32),
        compiler_params=_params("arbitrary", "arbitrary"),
        name="ada",
    )(c16, w_ada, b_ada.reshape(depth, 1, n6))


def _norm_mod_kernel(x_ref, g_ref, sh_ref, sc_ref, o_ref):
    x = x_ref[...]
    y = x * lax.rsqrt(jnp.mean(x * x, axis=-1, keepdims=True) + EPS) * g_ref[...]
    o_ref[...] = (y * (1.0 + sc_ref[...]) + sh_ref[...]).astype(o_ref.dtype)


def _norm_mod(x, g, mod5, l, which, b_off, tm, tpb):
    m, d = x.shape
    mod_spec = lambda w: pl.BlockSpec((None, None, None, 1, d),
                                      lambda i: (l, w, b_off + i // tpb, 0, 0))
    return pl.pallas_call(
        _norm_mod_kernel,
        grid=(m // tm,),
        in_specs=[
            pl.BlockSpec((tm, d), lambda i: (i, 0)),
            pl.BlockSpec((None, 1, d), lambda i: (l, 0, 0)),
            mod_spec(which), mod_spec(which + 1),
        ],
        out_specs=pl.BlockSpec((tm, d), lambda i: (i, 0)),
        out_shape=jax.ShapeDtypeStruct((m, d), BF),
        compiler_params=_params("arbitrary"),
        name="norm_mod",
    )(x, g.reshape(g.shape[0], 1, d), mod5, mod5)


def _final_norm_kernel(x_ref, g_ref, o_ref):
    x = x_ref[...]
    o_ref[...] = x * lax.rsqrt(jnp.mean(x * x, axis=-1, keepdims=True) + EPS) * g_ref[...]


def _final_norm(x, g, tm):
    m, d = x.shape
    return pl.pallas_call(
        _final_norm_kernel,
        grid=(m // tm,),
        in_specs=[pl.BlockSpec((tm, d), lambda i: (i, 0)), pl.BlockSpec((1, d), lambda i: (0, 0))],
        out_specs=pl.BlockSpec((tm, d), lambda i: (i, 0)),
        out_shape=jax.ShapeDtypeStruct((m, d), F32),
        compiler_params=_params("arbitrary"),
        name="final_norm",
    )(x, g.reshape(1, d))


def _mm_kernel(*refs, k_sizes, res_gate, wt, second):
    n_a = len(k_sizes)
    per = n_a + (2 if res_gate else 0)
    w_ref = refs[0]
    groups = [refs[1:1 + per]] + ([refs[1 + per:1 + 2 * per]] if second else [])
    pos = 1 + per * len(groups)
    o_refs = refs[pos:pos + len(groups)]
    wb_ref = refs[pos + len(groups)]

    @pl.when(pl.program_id(1) == 0)
    def _():
        wb_ref[...] = w_ref[...].astype(BF)

    def product(group, o_ref):
        acc = None
        k0 = 0
        for a_ref, ks in zip(group[:n_a], k_sizes):
            if wt:
                part = lax.dot_general(a_ref[...], wb_ref[:, k0:k0 + ks], (((1,), (1,)), ((), ())),
                                       preferred_element_type=F32)
            else:
                part = jnp.dot(a_ref[...], wb_ref[k0:k0 + ks, :], preferred_element_type=F32)
            acc = part if acc is None else acc + part
            k0 += ks
        if res_gate:
            acc = group[n_a][...] + group[n_a + 1][...] * acc
        o_ref[...] = acc.astype(o_ref.dtype)

    product(groups[0], o_refs[0])
    if second:
        @pl.when(pl.program_id(1) == pl.num_programs(1) - 1)
        def _():
            product(groups[1], o_refs[1])


def _mm(a_list, w, l, n, tm, tn, wt=False, res=None, gate_spec=None, gate=None, second=None, name="mm"):
    m = a_list[0].shape[0]
    k_sizes = tuple(a.shape[1] for a in a_list)
    k = sum(k_sizes)
    assert w.shape[2 if wt else 1] == k and n % tn == 0 and m % tm == 0
    if wt:
        in_specs = [pl.BlockSpec((None, tn, k), lambda j, i: (l, j, 0))]
    else:
        in_specs = [pl.BlockSpec((None, k, tn), lambda j, i: (l, 0, j))]
    args = [w]
    in_specs += [pl.BlockSpec((tm, ks), lambda j, i: (i, 0)) for ks in k_sizes]
    args += list(a_list)
    if res is not None:
        in_specs += [pl.BlockSpec((tm, tn), lambda j, i: (i, j)), gate_spec]
        args += [res, gate]
    out_specs = [pl.BlockSpec((tm, tn), lambda j, i: (i, j))]
    out_shape = [jax.ShapeDtypeStruct((m, n), F32)]
    if second is not None:
        a_list2, res2, gate2 = second
        m2 = a_list2[0].shape[0]
        in_specs += [pl.BlockSpec((m2, ks), lambda j, i: (0, 0)) for ks in k_sizes]
        args += list(a_list2)
        if res is not None:
            in_specs += [pl.BlockSpec((m2, tn), lambda j, i: (0, j))] * 2
            args += [res2, gate2]
        out_specs.append(pl.BlockSpec((m2, tn), lambda j, i: (0, j)))
        out_shape.append(jax.ShapeDtypeStruct((m2, n), F32))
    outs = pl.pallas_call(
        functools.partial(_mm_kernel, k_sizes=k_sizes, res_gate=res is not None, wt=wt, second=second is not None),
        grid=(n // tn, m // tm),
        in_specs=in_specs,
        out_specs=out_specs,
        out_shape=out_shape,
        scratch_shapes=[pltpu.VMEM((tn, k) if wt else (k, tn), BF)],
        compiler_params=_params("arbitrary", "arbitrary"),
        name=name,
    )(*args)
    return outs if second is not None else outs[0]


def _conv3(u, p, w):
    row = lax.broadcasted_iota(jnp.int32, u.shape, 0)
    um1 = jnp.where(row == 0, p[1:2], pltpu.roll(u, 1, 0))
    um2 = jnp.where(row == 0, p[0:1], jnp.where(row == 1, p[1:2], pltpu.roll(u, 2, 0)))
    return w[0:1] * um2 + w[1:2] * um1 + w[2:3] * u


def _prev_rows(i, tpb, prev_ref, carry_ref):
    if tpb == 1:
        return prev_ref[...]
    return jnp.where(i % tpb == 0, prev_ref[...], carry_ref[6:8, :])


def _mixer_a_kernel(ain_ref, ab_ref, ac_ref, w_ref, prev_ref, o_ref, st_ref, carry_ref, *, tm, tpb):
    i = pl.program_id(0)
    u = ac_ref[...] * ain_ref[...]
    p = _prev_rows(i, tpb, prev_ref, carry_ref)
    o_ref[...] = (ab_ref[...] * _conv3(u, p, w_ref[...])).astype(o_ref.dtype)
    st_ref[...] = u[tm - 2:tm]
    if tpb > 1:
        carry_ref[...] = u[tm - 8:tm]


def _mixer_a(z, sconv_w, l, prev, tm, tpb):
    m = z.shape[0]
    c = prev.shape[-1]
    nb = prev.shape[0]
    return pl.pallas_call(
        functools.partial(_mixer_a_kernel, tm=tm, tpb=tpb),
        grid=(m // tm,),
        in_specs=[
            pl.BlockSpec((tm, c), lambda i: (i, 0)),
            pl.BlockSpec((tm, c), lambda i: (i, 1)),
            pl.BlockSpec((tm, c), lambda i: (i, 2)),
            pl.BlockSpec((None, 3, c), lambda i: (l, 0, 0)),
            pl.BlockSpec((None, 2, c), lambda i: (i // tpb, 0, 0)),
        ],
        out_specs=[
            pl.BlockSpec((tm, c), lambda i: (i, 0)),
            pl.BlockSpec((None, 2, c), lambda i: (i // tpb, 0, 0)),
        ],
        out_shape=[jax.ShapeDtypeStruct((m, c), BF), jax.ShapeDtypeStruct((nb, 2, c), F32)],
        scratch_shapes=[pltpu.VMEM((8, c), F32)],
        compiler_params=_params("arbitrary"),
        name="mixer_a",
    )(z, z, z, sconv_w, prev)


def _ffn_act_kernel(a_ref, g_ref, w_ref, prev_ref, o_ref, st_ref, carry_ref, *, tm, tpb):
    i = pl.program_id(1)
    a = a_ref[...]
    p = _prev_rows(i, tpb, prev_ref, carry_ref)
    o_ref[...] = (_silu(_conv3(a, p, w_ref[...])) * g_ref[...]).astype(o_ref.dtype)
    st_ref[...] = a[tm - 2:tm]
    if tpb > 1:
        carry_ref[...] = a[tm - 8:tm]


def _ffn_act(za, zg, conv_w, l, prev, tm, tpb, tn):
    m = za.shape[0]
    dff = prev.shape[-1]
    nb = prev.shape[0]
    nj = dff // tn
    return pl.pallas_call(
        functools.partial(_ffn_act_kernel, tm=tm, tpb=tpb),
        grid=(nj, m // tm),
        in_specs=[
            pl.BlockSpec((tm, tn), lambda j, i: (i, j)),
            pl.BlockSpec((tm, tn), lambda j, i: (i, j)),
            pl.BlockSpec((None, 3, tn), lambda j, i: (l, 0, j)),
            pl.BlockSpec((None, 2, tn), lambda j, i: (i // tpb, 0, j)),
        ],
        out_specs=[
            pl.BlockSpec((tm, tn), lambda j, i: (i, j)),
            pl.BlockSpec((None, 2, tn), lambda j, i: (i // tpb, 0, j)),
        ],
        out_shape=[jax.ShapeDtypeStruct((m, dff), BF), jax.ShapeDtypeStruct((nb, 2, dff), F32)],
        scratch_shapes=[pltpu.VMEM((8, tn), F32)],
        compiler_params=_params("arbitrary", "arbitrary"),
        name="ffn_act",
    )(za, zg, conv_w, prev)


def _ffn_up_kernel(h_ref, wa_ref, wg_ref, cw_ref, prev_ref, h2_ref, o_ref, st_ref, a2_ref, g2_ref,
                   wab_ref, wgb_ref, carry_ref, *, tm, tpb):
    i = pl.program_id(1)

    @pl.when(i == 0)
    def _():
        wab_ref[...] = wa_ref[...].astype(BF)
        wgb_ref[...] = wg_ref[...].astype(BF)

    h = h_ref[...]
    a = jnp.dot(h, wab_ref[...], preferred_element_type=F32)
    g = jnp.dot(h, wgb_ref[...], preferred_element_type=F32)
    p = _prev_rows(i, tpb, prev_ref, carry_ref)
    o_ref[...] = (_silu(_conv3(a, p, cw_ref[...])) * g).astype(o_ref.dtype)
    st_ref[...] = a[tm - 2:tm]
    if tpb > 1:
        carry_ref[...] = a[tm - 8:tm]

    @pl.when(i == pl.num_programs(1) - 1)
    def _():
        a2_ref[...] = jnp.dot(h2_ref[...], wab_ref[...], preferred_element_type=F32)
        g2_ref[...] = jnp.dot(h2_ref[...], wgb_ref[...], preferred_element_type=F32)


def _ffn_up(h, w_up, conv_w, l, prev, tm, tpb, tn, h2):
    m, d = h.shape
    m2 = h2.shape[0]
    dff = prev.shape[-1]
    nb = prev.shape[0]
    nj = dff // tn
    small = pl.BlockSpec((m2, tn), lambda j, i: (0, j))
    return pl.pallas_call(
        functools.partial(_ffn_up_kernel, tm=tm, tpb=tpb),
        grid=(nj, m // tm),
        in_specs=[
            pl.BlockSpec((tm, d), lambda j, i: (i, 0)),
            pl.BlockSpec((None, d, tn), lambda j, i: (l, 0, j)),
            pl.BlockSpec((None, d, tn), lambda j, i: (l, 0, j + nj)),
            pl.BlockSpec((None, 3, tn), lambda j, i: (l, 0, j)),
            pl.BlockSpec((None, 2, tn), lambda j, i: (i // tpb, 0, j)),
            pl.BlockSpec((m2, d), lambda j, i: (0, 0)),
        ],
        out_specs=[
            pl.BlockSpec((tm, tn), lambda j, i: (i, j)),
            pl.BlockSpec((None, 2, tn), lambda j, i: (i // tpb, 0, j)),
            small, small,
        ],
        out_shape=[jax.ShapeDtypeStruct((m, dff), BF), jax.ShapeDtypeStruct((nb, 2, dff), F32),
                   jax.ShapeDtypeStruct((m2, dff), F32), jax.ShapeDtypeStruct((m2, dff), F32)],
        scratch_shapes=[pltpu.VMEM((d, tn), BF), pltpu.VMEM((d, tn), BF), pltpu.VMEM((8, tn), F32)],
        compiler_params=_params("arbitrary", "arbitrary"),
        name="ffn_up_fused",
    )(h, w_up, w_up, conv_w, prev, h2)


def _odd_post_kernel(ca_ref, cg_ref, du_ref, dv_ref, cw_ref, cb_ref, clg_ref, clb_ref, dlg_ref,
                     dlb_ref, ws_ref, bst_ref, prev_ref, o_ref, st_ref, v_ref, cbuf, wbuf, *, tm, tpb):
    i = pl.program_id(0)
    cw = cw_ref.shape[-1]
    c = ca_ref[...] * _sigmoid(cg_ref[...])
    if tpb == 1:
        cbuf[2:32, :] = prev_ref[...]
    else:
        @pl.when(i % tpb == 0)
        def _():
            cbuf[2:32, :] = prev_ref[...]

        @pl.when(i % tpb != 0)
        def _():
            cbuf[0:32, :] = cbuf[tm:tm + 32, :]
    cbuf[32:32 + tm, :] = c
    acc = None
    for r in range(8):
        taps = range(r, CCONV_W, 8)
        rows = tm + 8 * (len(taps) - 1)
        wbuf[0:rows, :] = cbuf[2 + r:2 + r + rows, :]
        for j, k in enumerate(taps):
            term = cw_ref[k:k + 1, :] * wbuf[8 * j:8 * j + tm, :]
            acc = term if acc is None else acc + term
    st_ref[...] = cbuf[tm + 2:tm + 32, :]
    o_ref[:, 0:cw] = _silu(_layernorm(acc + cb_ref[...], clg_ref[...], clb_ref[...])).astype(o_ref.dtype)

    u = _gelu_tanh(du_ref[...])
    v = _layernorm(_gelu_tanh(dv_ref[...]), dlg_ref[...], dlb_ref[...])
    v_ref[...] = v
    gw = cw // D_GROUPS
    trow = lax.broadcasted_iota(jnp.int32, (D_CHUNK, D_CHUNK), 0)
    tcol = lax.broadcasted_iota(jnp.int32, (D_CHUNK, D_CHUNK), 1)
    rows = min(tm, D_CHUNK)
    for ch in range(max(1, tm // D_CHUNK)):
        r0 = ch * D_CHUNK
        vch = v[r0:r0 + rows]
        if rows < D_CHUNK:
            vch = jnp.concatenate([vch, jnp.zeros((D_CHUNK - rows, cw), F32)], axis=0)
        vch = vch.astype(BF)
        for g in range(D_GROUPS):
            wg = jnp.where(tcol <= trow, ws_ref[g], 0.0).astype(BF)
            zz = jnp.dot(wg, vch[:, g * gw:(g + 1) * gw], preferred_element_type=F32)
            zz = zz + bst_ref[:, g:g + 1]
            o_ref[r0:r0 + rows, cw + g * gw:cw + (g + 1) * gw] = (
                u[r0:r0 + rows, g * gw:(g + 1) * gw] * zz[0:rows]).astype(o_ref.dtype)


def _odd_post(z, i_odd, prev, cconv_w, cconv_b, c_ln_g, c_ln_b, d_ln_g, d_ln_b, d_ws, d_bs_t, tm, tpb):
    m = z.shape[0]
    c = prev.shape[-1]
    nb = prev.shape[0]
    vec = lambda: pl.BlockSpec((None, 1, c), lambda i: (i_odd, 0, 0))
    r3 = lambda a: a.reshape(a.shape[0], 1, c)
    return pl.pallas_call(
        functools.partial(_odd_post_kernel, tm=tm, tpb=tpb),
        grid=(m // tm,),
        in_specs=[
            pl.BlockSpec((tm, c), lambda i: (i, 0)),
            pl.BlockSpec((tm, c), lambda i: (i, 1)),
            pl.BlockSpec((tm, c), lambda i: (i, 2)),
            pl.BlockSpec((tm, c), lambda i: (i, 3)),
            pl.BlockSpec((None, CCONV_W, c), lambda i: (i_odd, 0, 0)),
            vec(), vec(), vec(), vec(), vec(),
            pl.BlockSpec((None, D_GROUPS, D_CHUNK, D_CHUNK), lambda i: (i_odd, 0, 0, 0)),
            pl.BlockSpec((None, D_CHUNK, D_GROUPS), lambda i: (i_odd, 0, 0)),
            pl.BlockSpec((None, CCONV_W - 1, c), lambda i: (i // tpb, 0, 0)),
        ],
        out_specs=[
            pl.BlockSpec((tm, 2 * c), lambda i: (i, 0)),
            pl.BlockSpec((None, CCONV_W - 1, c), lambda i: (i // tpb, 0, 0)),
            pl.BlockSpec((tm, c), lambda i: (i, 0)),
        ],
        out_shape=[jax.ShapeDtypeStruct((m, 2 * c), BF),
                   jax.ShapeDtypeStruct((nb, CCONV_W - 1, c), F32),
                   jax.ShapeDtypeStruct((m, c), F32)],
        scratch_shapes=[pltpu.VMEM((32 + tm, c), F32), pltpu.VMEM((24 + tm, c), F32)],
        compiler_params=_params("arbitrary"),
        name="odd_post",
    )(z, z, z, z, cconv_w, r3(cconv_b), r3(c_ln_g), r3(c_ln_b), r3(d_ln_g), r3(d_ln_b), d_ws, d_bs_t, prev)


def _rope128(x, cos, sin_signed):
    lane = lax.broadcasted_iota(jnp.int32, x.shape, 1)
    swapped = jnp.where((lane & (HEAD_DIM - 1)) < HEAD_DIM // 2,
                        pltpu.roll(x, LANES - HEAD_DIM // 2, 1), pltpu.roll(x, HEAD_DIM // 2, 1))
    return x * cos + swapped * sin_signed


def _rope_slab(ref, c0, width, cos, sin_signed):
    return jnp.concatenate(
        [_rope128(ref[:, c0 + k * LANES:c0 + (k + 1) * LANES], cos, sin_signed) for k in range(width // LANES)],
        axis=1)


def _rope_prompt_kernel(zq_ref, zn_ref, zw_ref, cos_ref, sin_ref, qt_ref, nsat_ref, wint_ref, kv_ref, ksa_ref,
                        kc_ref, vc_ref, *, ts, tpb):
    cos, sin = cos_ref[...], sin_ref[...]
    kvw = KV_HEADS * HEAD_DIM
    scale = Q_SCALE
    for k in range(N_HEADS * HEAD_DIM // LANES):
        rt = (_rope128(zq_ref[:, k * LANES:(k + 1) * LANES], cos, sin) * scale).T
        qt_ref[2 * k] = rt[:HEAD_DIM].astype(qt_ref.dtype)
        qt_ref[2 * k + 1] = rt[HEAD_DIM:].astype(qt_ref.dtype)
    k_cmp = _rope_slab(zn_ref, 0, kvw, cos, sin)
    v_cmp = zn_ref[:, kvw:2 * kvw]
    k_sel = _rope_slab(zn_ref, 2 * kvw, kvw, cos, sin)
    v_sel = zn_ref[:, 3 * kvw:4 * kvw]
    k_win = _rope_slab(zw_ref, 0, kvw, cos, sin)
    v_win = zw_ref[:, kvw:2 * kvw]
    for t, slab in enumerate((k_cmp, v_cmp, k_sel, v_sel)):
        nsat_ref[t] = slab.T.reshape(KV_HEADS, HEAD_DIM, ts)
    for t, slab in enumerate((k_win, v_win)):
        wint_ref[t] = slab.T.reshape(KV_HEADS, HEAD_DIM, ts)
    for t, slab in enumerate((v_sel, k_win, v_win)):
        for g in range(KV_HEADS):
            kv_ref[t, g] = slab[:, g * HEAD_DIM:(g + 1) * HEAD_DIM].astype(kv_ref.dtype)
    row0 = (pl.program_id(0) % tpb) * ts
    blk_of_row = (row0 + lax.broadcasted_iota(jnp.int32, (ts, HEAD_DIM), 0)) // NSA_BLOCK
    onehot = jnp.where(blk_of_row == lax.broadcasted_iota(jnp.int32, (ts, HEAD_DIM), 1), 1.0, 0.0)
    for g in range(KV_HEADS):
        ksa_ref[g] = jnp.concatenate([k_sel[:, g * HEAD_DIM:(g + 1) * HEAD_DIM], onehot], axis=1).astype(ksa_ref.dtype)
    nblk = ts // NSA_BLOCK
    kc = jnp.sum(k_cmp.reshape(nblk, NSA_BLOCK, kvw), axis=1) * (1.0 / NSA_BLOCK)
    vc = jnp.sum(v_cmp.reshape(nblk, NSA_BLOCK, kvw), axis=1) * (1.0 / NSA_BLOCK)
    for g in range(KV_HEADS):
        kc_ref[g] = kc[:, g * HEAD_DIM:(g + 1) * HEAD_DIM]
        vc_ref[g] = vc[:, g * HEAD_DIM:(g + 1) * HEAD_DIM]


def _rope_prompt(z, cos, sin, batch, seq, ts):
    tpb = seq // ts
    qw = N_HEADS * HEAD_DIM
    nblk = ts // NSA_BLOCK
    assert seq // NSA_BLOCK <= HEAD_DIM
    return pl.pallas_call(
        functools.partial(_rope_prompt_kernel, ts=ts, tpb=tpb),
        grid=(batch * tpb,),
        in_specs=[
            pl.BlockSpec((ts, qw), lambda i: (i, 3)),
            pl.BlockSpec((ts, qw), lambda i: (i, 4)),
            pl.BlockSpec((ts, qw // 2), lambda i: (i, 10)),
            pl.BlockSpec((ts, LANES), lambda i: (i % tpb, 0)),
            pl.BlockSpec((ts, LANES), lambda i: (i % tpb, 0)),
        ],
        out_specs=[
            pl.BlockSpec((None, N_HEADS, HEAD_DIM, ts), lambda i: (i // tpb, 0, 0, i % tpb)),
            pl.BlockSpec((None, 4, KV_HEADS, HEAD_DIM, ts), lambda i: (i // tpb, 0, 0, 0, i % tpb)),
            pl.BlockSpec((None, 2, KV_HEADS, HEAD_DIM, ts), lambda i: (i // tpb, 0, 0, 0, i % tpb)),
            pl.BlockSpec((None, 3, KV_HEADS, ts, HEAD_DIM), lambda i: (i // tpb, 0, 0, i % tpb, 0)),
            pl.BlockSpec((None, KV_HEADS, ts, 2 * HEAD_DIM), lambda i: (i // tpb, 0, i % tpb, 0)),
            pl.BlockSpec((None, KV_HEADS, nblk, HEAD_DIM), lambda i: (i // tpb, 0, i % tpb, 0)),
            pl.BlockSpec((None, KV_HEADS, nblk, HEAD_DIM), lambda i: (i // tpb, 0, i % tpb, 0)),
        ],
        out_shape=[
            jax.ShapeDtypeStruct((batch, N_HEADS, HEAD_DIM, seq), BF),
            jax.ShapeDtypeStruct((batch, 4, KV_HEADS, HEAD_DIM, seq), F32),
            jax.ShapeDtypeStruct((batch, 2, KV_HEADS, HEAD_DIM, seq), F32),
            jax.ShapeDtypeStruct((batch, 3, KV_HEADS, seq, HEAD_DIM), BF),
            jax.ShapeDtypeStruct((batch, KV_HEADS, seq, 2 * HEAD_DIM), BF),
            jax.ShapeDtypeStruct((batch, KV_HEADS, seq // NSA_BLOCK, HEAD_DIM), F32),
            jax.ShapeDtypeStruct((batch, KV_HEADS, seq // NSA_BLOCK, HEAD_DIM), F32),
        ],
        compiler_params=_params("arbitrary"),
        name="rope_prompt",
    )(z, z, z, cos, sin)


def _rope_sample_kernel(zq_ref, zn_ref, zw_ref, cos_ref, sin_ref, q_ref, nsa_ref, win_ref):
    cos, sin = cos_ref[...], sin_ref[...]
    kvw = KV_HEADS * HEAD_DIM
    scale = Q_SCALE
    for k in range(N_HEADS * HEAD_DIM // LANES):
        r = _rope128(zq_ref[:, k * LANES:(k + 1) * LANES], cos, sin) * scale
        q_ref[2 * k] = r[:, :HEAD_DIM]
        q_ref[2 * k + 1] = r[:, HEAD_DIM:]
    nsa_ref[:, 0:kvw] = _rope_slab(zn_ref, 0, kvw, cos, sin)
    nsa_ref[:, kvw:2 * kvw] = zn_ref[:, kvw:2 * kvw]
    nsa_ref[:, 2 * kvw:3 * kvw] = _rope_slab(zn_ref, 2 * kvw, kvw, cos, sin)
    nsa_ref[:, 3 * kvw:4 * kvw] = zn_ref[:, 3 * kvw:4 * kvw]
    win_ref[:, 0:kvw] = _rope_slab(zw_ref, 0, kvw, cos, sin)
    win_ref[:, kvw:2 * kvw] = zw_ref[:, kvw:2 * kvw]


def _rope_sample(z, cos, sin, batch, seq):
    qw = N_HEADS * HEAD_DIM
    return pl.pallas_call(
        _rope_sample_kernel,
        grid=(batch,),
        in_specs=[
            pl.BlockSpec((seq, qw), lambda i: (i, 3)),
            pl.BlockSpec((seq, qw), lambda i: (i, 4)),
            pl.BlockSpec((seq, qw // 2), lambda i: (i, 10)),
            pl.BlockSpec((seq, LANES), lambda i: (0, 0)),
            pl.BlockSpec((seq, LANES), lambda i: (0, 0)),
        ],
        out_specs=[
            pl.BlockSpec((None, N_HEADS, seq, HEAD_DIM), lambda i: (i, 0, 0, 0)),
            pl.BlockSpec((seq, qw), lambda i: (i, 0)),
            pl.BlockSpec((seq, qw // 2), lambda i: (i, 0)),
        ],
        out_shape=[
            jax.ShapeDtypeStruct((batch, N_HEADS, seq, HEAD_DIM), F32),
            jax.ShapeDtypeStruct((batch * seq, qw), F32),
            jax.ShapeDtypeStruct((batch * seq, qw // 2), F32),
        ],
        compiler_params=_params("arbitrary"),
        name="rope_sample",
    )(z, z, z, cos, sin)


def _cmp_branch(kc, vc, q, qpos, nq):
    nb = kc.shape[0]
    st = lax.dot_general(kc.astype(BF), q, (((1,), (1,)), ((), ())), preferred_element_type=F32)
    blk = lax.broadcasted_iota(jnp.int32, st.shape, 0)
    ok = (blk + 1) * NSA_BLOCK <= qpos + 1
    sm = jnp.where(ok, st, NEG)
    mx = jnp.max(sm, axis=0, keepdims=True)
    e = jnp.where(ok, jnp.exp2(sm - mx), 0.0)
    den = jnp.sum(e, axis=0, keepdims=True)
    pt = e / jnp.where(den > 0.0, den, 1.0)
    o_cmp = lax.dot_general(pt.astype(BF), vc.astype(BF), (((0,), (0,)), ((), ())), preferred_element_type=F32)
    imp = pt[:, 0:nq]
    for r in range(1, GQA):
        imp = imp + pt[:, r * nq:(r + 1) * nq]
    return o_cmp, imp


def _importance(imp, qpos_q):
    blk = lax.broadcasted_iota(jnp.int32, imp.shape, 0)
    cur = qpos_q // NSA_BLOCK
    forced = (blk == 0) | (blk == cur) | (blk == cur - 1)
    imp = jnp.where(forced, GQA + 1.0, imp)
    return jnp.where(blk <= cur, imp, -1.0)


def _select_topk(imp_ref, nb):
    imp = imp_ref[...]
    blk = lax.broadcasted_iota(jnp.int32, imp.shape, 0)

    def body(i, rank):
        row = imp_ref[pl.ds(i, 1), :]
        ahead = (row > imp) | ((row == imp) & (i < blk))
        return rank + jnp.where(ahead, 1.0, 0.0)

    rank = lax.fori_loop(0, nb, body, jnp.zeros(imp.shape, F32), unroll=8)
    return jnp.where(rank < float(N_SEL), 1.0, 0.0)


def _expand_blocks(sel_t, n_keys, first_block):
    nbl = sel_t.shape[0]
    kb = lax.broadcasted_iota(jnp.int32, (nbl, n_keys), 1) // NSA_BLOCK + first_block
    nn = lax.broadcasted_iota(jnp.int32, (nbl, n_keys), 0)
    e = jnp.where(kb == nn, 1.0, 0.0).astype(BF)
    return lax.dot_general(sel_t.astype(BF), e, (((0,), (0,)), ((), ())), preferred_element_type=F32)


def _online_update(s, valid, vt, m_ref, l_ref, acc_ref, idx, v_rows=None):
    nk = s.shape[1]
    s = jnp.where(valid, s, NEG)
    m_prev = m_ref[idx]
    m_new = jnp.maximum(m_prev, jnp.max(s, axis=1, keepdims=True))
    alpha = jnp.exp2(m_prev - m_new)
    p = jnp.where(valid, jnp.exp2(s - jnp.concatenate([m_new] * (nk // LANES), axis=1)), 0.0)
    l_ref[idx] = alpha * l_ref[idx] + jnp.sum(p, axis=1, keepdims=True)
    if v_rows is None:
        pv = lax.dot_general(p.astype(BF), vt, (((1,), (1,)), ((), ())), preferred_element_type=F32)
    else:
        pv = jnp.dot(p.astype(BF), v_rows, preferred_element_type=F32)
    acc_ref[idx] = alpha[:, :HEAD_DIM] * acc_ref[idx] + pv
    m_ref[idx] = m_new


def _online_update_t(k_rows, qt, bias, v_rows, m_ref, l_ref, acc_ref, idx):
    st = jnp.dot(k_rows, qt, preferred_element_type=F32)
    nq = qt.shape[1] // GQA
    ps, alphas = [], []
    for r in range(GQA):
        cols = slice(r * nq, (r + 1) * nq)
        s_r = st[:, cols] if bias is None else st[:, cols] + bias
        m_prev = m_ref[idx, :, cols]
        m_new = jnp.maximum(m_prev, jnp.max(s_r, axis=0, keepdims=True))
        alpha = jnp.exp2(m_prev - m_new)
        p_r = jnp.exp2(s_r - m_new)
        l_ref[idx, :, cols] = alpha * l_ref[idx, :, cols] + jnp.sum(p_r, axis=0, keepdims=True)
        m_ref[idx, :, cols] = m_new
        ps.append(p_r.astype(BF))
        alphas.append(alpha)
    pv = lax.dot_general(v_rows, jnp.concatenate(ps, axis=1), (((0,), (0,)), ((), ())),
                         preferred_element_type=F32)
    acc_ref[idx] = jnp.concatenate(alphas, axis=1) * acc_ref[idx] + pv


def _nsa_prompt_kernel(qt_ref, kc_ref, vc_ref, ks_ref, vs_ref, kw_ref, vw_ref, gl_ref, o_ref,
                       imp_ref, m_ref, l_ref, acc_ref, *, tq, seq):
    qi = pl.program_id(2)
    q0 = qi * tq
    rq = GQA * tq
    nb = seq // NSA_BLOCK
    bpt = tq // NSA_BLOCK
    qt = jnp.concatenate([qt_ref[r] for r in range(GQA)], axis=1)

    st = jnp.dot(kc_ref[...].astype(BF), qt, preferred_element_type=F32)
    blk = lax.broadcasted_iota(jnp.int32, (nb, rq), 0)
    qpos_r = q0 + (lax.broadcasted_iota(jnp.int32, (nb, rq), 1) & (tq - 1))
    ok = (blk + 1) * NSA_BLOCK <= qpos_r + 1
    sm = jnp.where(ok, st, NEG)
    e = jnp.where(ok, jnp.exp2(sm - jnp.max(sm, axis=0, keepdims=True)), 0.0)
    den = jnp.sum(e, axis=0, keepdims=True)
    pt = e / jnp.where(den > 0.0, den, 1.0)
    o_cmp = lax.dot_general(vc_ref[...].astype(BF), pt.astype(BF), (((0,), (0,)), ((), ())),
                            preferred_element_type=F32)
    imp = pt[:, 0:tq]
    for r in range(1, GQA):
        imp = imp + pt[:, r * tq:(r + 1) * tq]
    imp_ref[...] = _importance(imp, q0 + lax.broadcasted_iota(jnp.int32, (nb, tq), 1))
    sel_bias = ((_select_topk(imp_ref, nb) - 1.0) * (-NEG)).astype(BF)
    q_sel = jnp.concatenate([qt, jnp.concatenate([sel_bias] * GQA, axis=1), jnp.zeros((HEAD_DIM - nb, rq), BF)],
                            axis=0)

    m_ref[...] = jnp.full(m_ref.shape, NEG, F32)
    l_ref[...] = jnp.zeros(l_ref.shape, F32)
    acc_ref[...] = jnp.zeros(acc_ref.shape, F32)

    krow = lax.broadcasted_iota(jnp.int32, (tq, tq), 0)
    qcol = lax.broadcasted_iota(jnp.int32, (tq, tq), 1)

    def rows_of(c):
        return pl.ds(pl.multiple_of(c * tq, tq), tq)

    def sel_chunk(c):
        _online_update_t(ks_ref[rows_of(c), :], q_sel, None, vs_ref[rows_of(c), :], m_ref, l_ref, acc_ref, 0)

    def sel_pair(j, carry):
        sel_chunk(2 * j)
        sel_chunk(2 * j + 1)
        return carry

    n_full = (WINDOW - tq) // tq
    n_far = jnp.maximum(qi - (n_full + 1), 0)
    lax.fori_loop(0, n_far // 2, sel_pair, 0)

    @pl.when(n_far % 2 == 1)
    def _():
        sel_chunk(n_far - 1)
    for rel in range(n_full + 1, 0, -1):
        @pl.when(qi >= rel)
        def _(rel=rel):
            c = qi - rel
            sel_chunk(c)
            bias = None if rel <= n_full else jnp.where(krow > qcol + (rel * tq - WINDOW), 0.0, NEG)
            _online_update_t(kw_ref[rows_of(c), :], qt, bias, vw_ref[rows_of(c), :], m_ref, l_ref, acc_ref, 1)
    causal_bias = jnp.where(krow <= qcol, 0.0, NEG)
    _online_update_t(ks_ref[rows_of(qi), :], q_sel, causal_bias, vs_ref[rows_of(qi), :], m_ref, l_ref, acc_ref, 0)
    _online_update_t(kw_ref[rows_of(qi), :], qt, causal_bias, vw_ref[rows_of(qi), :], m_ref, l_ref, acc_ref, 1)

    o_sel = acc_ref[0] * (1.0 / l_ref[0])
    o_win = acc_ref[1] * (1.0 / l_ref[1])
    gate_t = _sigmoid(gl_ref[...]).T
    outs = []
    for r in range(GQA):
        cols = slice(r * tq, (r + 1) * tq)
        o_t = (gate_t[3 * r:3 * r + 1] * o_cmp[:, cols] + gate_t[3 * r + 1:3 * r + 2] * o_sel[:, cols]
               + gate_t[3 * r + 2:3 * r + 3] * o_win[:, cols])
        outs.append(o_t.T)
    o_ref[...] = jnp.concatenate(outs, axis=1).astype(o_ref.dtype)


def _nsa_prompt(qt, kc, vc, ksa, kv, gl, batch, seq, tq):
    nq = seq // tq
    nb = seq // NSA_BLOCK
    rq = GQA * tq
    assert WINDOW % tq == 0
    kv_spec = lambda t: pl.BlockSpec((None, None, None, seq, HEAD_DIM), lambda b, g, i: (b, t, g, 0, 0))
    cmp_spec = pl.BlockSpec((None, None, nb, HEAD_DIM), lambda b, g, i: (b, g, 0, 0))
    return pl.pallas_call(
        functools.partial(_nsa_prompt_kernel, tq=tq, seq=seq),
        grid=(batch, KV_HEADS, nq),
        in_specs=[
            pl.BlockSpec((None, GQA, HEAD_DIM, tq), lambda b, g, i: (b, g, 0, i)),
            cmp_spec, cmp_spec,
            pl.BlockSpec((None, None, seq, 2 * HEAD_DIM), lambda b, g, i: (b, g, 0, 0)),
            kv_spec(0), kv_spec(1), kv_spec(2),
            pl.BlockSpec((tq, LANES), lambda b, g, i: (b * nq + i, g)),
        ],
        out_specs=pl.BlockSpec((tq, GQA * HEAD_DIM), lambda b, g, i: (b * nq + i, g)),
        out_shape=jax.ShapeDtypeStruct((batch * seq, N_HEADS * HEAD_DIM), BF),
        scratch_shapes=[
            pltpu.VMEM((nb, tq), F32),
            pltpu.VMEM((2, 1, rq), F32),
            pltpu.VMEM((2, 1, rq), F32),
            pltpu.VMEM((2, HEAD_DIM, rq), F32),
        ],
        compiler_params=_params("arbitrary", "arbitrary", "arbitrary"),
        name="nsa_prompt",
    )(qt, kc, vc, ksa, kv, kv, kv, gl)


def _cmp_means_kernel(pt_ref, *refs, n_pages):
    page_refs = refs[:n_pages]
    kc_ref, vc_ref = refs[n_pages], refs[n_pages + 1]
    bpp = PAGE_SIZE // NSA_BLOCK
    kvw = KV_HEADS * HEAD_DIM
    for t, out in enumerate((kc_ref, vc_ref)):
        x = jnp.concatenate([page_refs[p][t].reshape(kvw, PAGE_SIZE).T for p in range(n_pages)], axis=0)
        out[...] = jnp.sum(x.reshape(n_pages * bpp, NSA_BLOCK, kvw), axis=1) * (1.0 / NSA_BLOCK)


def _cmp_means(cache_t, layer, page_table, n_pages):
    batch, ppb = page_table.shape
    kvw = KV_HEADS * HEAD_DIM
    bpp = PAGE_SIZE // NSA_BLOCK
    steps = ppb // n_pages

    def page_spec(p):
        return pl.BlockSpec((None, None, 2, KV_HEADS, HEAD_DIM, PAGE_SIZE),
                            lambda b, s, pt: (layer, pt[b, s * n_pages + p], 0, 0, 0, 0))

    out_spec = pl.BlockSpec((None, n_pages * bpp, kvw), lambda b, s, pt: (b, s, 0))
    return pl.pallas_call(
        functools.partial(_cmp_means_kernel, n_pages=n_pages),
        grid_spec=pltpu.PrefetchScalarGridSpec(
            num_scalar_prefetch=1, grid=(batch, steps),
            in_specs=[page_spec(p) for p in range(n_pages)],
            out_specs=[out_spec, out_spec]),
        out_shape=[jax.ShapeDtypeStruct((batch, ppb * bpp, kvw), F32)] * 2,
        compiler_params=_params("arbitrary", "arbitrary"),
        name="cmp_means",
    )(page_table, *([cache_t] * n_pages))


def _nsa_sample_kernel(pt_ref, *refs, n_pages, past, s_new):
    (q_ref, kc_ref, vc_ref, new_ref, wnew_ref, wbuf_ref, gl_ref) = refs[:7]
    page_refs = refs[7:7 + n_pages]
    o_ref = refs[7 + n_pages]
    kcf, vcf, imp_ref, sel_ref, m_ref, l_ref, acc_ref, ocmp_ref = refs[8 + n_pages:]
    step = pl.program_id(1)
    nsteps = pl.num_programs(1)
    kvw = KV_HEADS * HEAD_DIM
    rq = GQA * s_new
    nbp = past // NSA_BLOCK
    nbf = kcf.shape[0]
    wlen = wbuf_ref.shape[-1]
    pad_rows = LANES - s_new

    def q_of(g):
        return q_ref[g * GQA:(g + 1) * GQA].reshape(rq, HEAD_DIM).astype(BF)

    def pad_keys(x):
        return jnp.concatenate([x, jnp.zeros((pad_rows, HEAD_DIM), F32)], axis=0).astype(BF)

    @pl.when(step == 0)
    def _():
        row8 = lax.broadcasted_iota(jnp.int32, (nbf - nbp, kvw), 0)
        for full, src, c0 in ((kcf, kc_ref, 0), (vcf, vc_ref, kvw)):
            full[0:nbp, :] = src[...]
            mean_new = jnp.sum(new_ref[:, c0:c0 + kvw], axis=0, keepdims=True) * (1.0 / NSA_BLOCK)
            full[nbp:nbf, :] = jnp.where(row8 == 0, mean_new, 0.0)
        col = lax.broadcasted_iota(jnp.int32, (nbf, rq), 1)
        qpos_q = past + lax.broadcasted_iota(jnp.int32, (nbf, s_new), 1)
        for g in range(KV_HEADS):
            lanes = slice(g * HEAD_DIM, (g + 1) * HEAD_DIM)
            o_cmp, imp = _cmp_branch(kcf[:, lanes], vcf[:, lanes], q_of(g), past + (col & (s_new - 1)), s_new)
            ocmp_ref[g] = o_cmp
            imp_ref[:, g * s_new:(g + 1) * s_new] = _importance(imp, qpos_q)
        sel = _select_topk(imp_ref, nbf)
        for g in range(KV_HEADS):
            sel_ref[g] = jnp.concatenate([sel[:, g * s_new:(g + 1) * s_new]] * GQA, axis=1)
        m_ref[...] = jnp.full(m_ref.shape, NEG, F32)
        l_ref[...] = jnp.zeros(l_ref.shape, F32)
        acc_ref[...] = jnp.zeros(acc_ref.shape, F32)

    nk = n_pages * PAGE_SIZE
    nbl = nk // NSA_BLOCK
    for g in range(KV_HEADS):
        kt = jnp.concatenate([page_refs[p][0, g] for p in range(n_pages)], axis=1).astype(BF)
        vt = jnp.concatenate([page_refs[p][1, g] for p in range(n_pages)], axis=1).astype(BF)
        sel_rows = sel_ref[g, pl.ds(pl.multiple_of(step * nbl, nbl), nbl), :]
        valid = _expand_blocks(sel_rows, nk, 0) > 0.5
        s = jnp.dot(q_of(g), kt, preferred_element_type=F32)
        _online_update(s, valid, vt, m_ref, l_ref, acc_ref, g)

    @pl.when(step == nsteps - 1)
    def _():
        tq_col = lax.broadcasted_iota(jnp.int32, (rq, LANES), 0) & (s_new - 1)
        tk = lax.broadcasted_iota(jnp.int32, (rq, LANES), 1)
        new_ok = (tk <= tq_col) & (tk < s_new)
        gate = _sigmoid(gl_ref[...])
        tail = nbf - 16
        for g in range(KV_HEADS):
            q = q_of(g)
            lane0 = 2 * kvw + g * HEAD_DIM
            k_new = pad_keys(new_ref[:, lane0:lane0 + HEAD_DIM])
            v_new = pad_keys(new_ref[:, lane0 + kvw:lane0 + kvw + HEAD_DIM])
            kb = lax.broadcasted_iota(jnp.int32, (16, LANES), 0)
            e = jnp.where(kb == nbp - tail, 1.0, 0.0).astype(BF)
            selx = lax.dot_general(sel_ref[g, tail:nbf, :].astype(BF), e, (((0,), (0,)), ((), ())),
                                   preferred_element_type=F32)
            s = lax.dot_general(q, k_new, (((1,), (1,)), ((), ())), preferred_element_type=F32)
            _online_update(s, (selx > 0.5) & new_ok, None, m_ref, l_ref, acc_ref, g, v_rows=v_new)
            o_sel = acc_ref[g] / l_ref[g][:, :HEAD_DIM]
            wl0 = g * HEAD_DIM
            kw_new = pad_keys(wnew_ref[:, wl0:wl0 + HEAD_DIM])
            vw_new = pad_keys(wnew_ref[:, kvw + wl0:kvw + wl0 + HEAD_DIM])
            sb = jnp.dot(q, wbuf_ref[0, g].astype(BF), preferred_element_type=F32)
            sn = lax.dot_general(q, kw_new, (((1,), (1,)), ((), ())), preferred_element_type=F32)
            jb = lax.broadcasted_iota(jnp.int32, (rq, wlen), 1)
            tq_b = lax.broadcasted_iota(jnp.int32, (rq, wlen), 0) & (s_new - 1)
            ok_b = (past - wlen + jb > past + tq_b - WINDOW) & (past - wlen + jb >= 0)
            sb = jnp.where(ok_b, sb, NEG)
            sn = jnp.where(new_ok, sn, NEG)
            mx = jnp.maximum(jnp.max(sb, axis=1, keepdims=True), jnp.max(sn, axis=1, keepdims=True))
            pb = jnp.where(ok_b, jnp.exp2(sb - mx), 0.0)
            pn = jnp.where(new_ok, jnp.exp2(sn - mx), 0.0)
            den = jnp.sum(pb, axis=1, keepdims=True) + jnp.sum(pn, axis=1, keepdims=True)
            o_win = (lax.dot_general(pb.astype(BF), wbuf_ref[1, g].astype(BF), (((1,), (1,)), ((), ())),
                                     preferred_element_type=F32)
                     + jnp.dot(pn.astype(BF), vw_new, preferred_element_type=F32)) / den
            o_cmp = ocmp_ref[g]
            for r in range(GQA):
                rows = slice(r * s_new, (r + 1) * s_new)
                c = g * LANES + 3 * r
                h = g * GQA + r
                o_ref[:, h * HEAD_DIM:(h + 1) * HEAD_DIM] = (
                    gate[:, c:c + 1] * o_cmp[rows] + gate[:, c + 1:c + 2] * o_sel[rows]
                    + gate[:, c + 2:c + 3] * o_win[rows]).astype(o_ref.dtype)


def _nsa_sample(cache_t, layer, page_table, q, kc, vc, nsa_new, win_new, wbuf_t, gl, n_pages, past, s_new):
    batch, ppb = page_table.shape
    kvw = KV_HEADS * HEAD_DIM
    qw = N_HEADS * HEAD_DIM
    nbp = past // NSA_BLOCK
    nbf = nbp + 8
    wlen = wbuf_t.shape[-1]
    rq = GQA * s_new
    steps = ppb // n_pages

    def page_spec(p):
        return pl.BlockSpec((None, None, 2, KV_HEADS, HEAD_DIM, PAGE_SIZE),
                            lambda b, s, pt: (layer, pt[b, s * n_pages + p], 1, 0, 0, 0))

    per_b = lambda shape: pl.BlockSpec((None,) + shape, lambda b, s, pt: (b,) + (0,) * len(shape))
    rows_b = lambda w: pl.BlockSpec((s_new, w), lambda b, s, pt: (b, 0))
    return pl.pallas_call(
        functools.partial(_nsa_sample_kernel, n_pages=n_pages, past=past, s_new=s_new),
        grid_spec=pltpu.PrefetchScalarGridSpec(
            num_scalar_prefetch=1, grid=(batch, steps),
            in_specs=[
                per_b((N_HEADS, s_new, HEAD_DIM)),
                per_b((nbp, kvw)), per_b((nbp, kvw)),
                rows_b(4 * kvw), rows_b(2 * kvw),
                pl.BlockSpec((None, None, 2, KV_HEADS, HEAD_DIM, wlen), lambda b, s, pt: (layer, b, 0, 0, 0, 0)),
                rows_b(KV_HEADS * LANES),
            ] + [page_spec(p) for p in range(n_pages)],
            out_specs=rows_b(qw),
            scratch_shapes=[
                pltpu.VMEM((nbf, kvw), F32), pltpu.VMEM((nbf, kvw), F32),
                pltpu.VMEM((nbf, KV_HEADS * s_new), F32),
                pltpu.VMEM((KV_HEADS, nbf, rq), F32),
                pltpu.VMEM((KV_HEADS, rq, LANES), F32),
                pltpu.VMEM((KV_HEADS, rq, LANES), F32),
                pltpu.VMEM((KV_HEADS, rq, HEAD_DIM), F32),
                pltpu.VMEM((KV_HEADS, rq, HEAD_DIM), F32),
            ]),
        out_shape=jax.ShapeDtypeStruct((batch * s_new, qw), BF),
        compiler_params=_params("arbitrary", "arbitrary"),
        name="nsa_sample",
    )(page_table, q, kc, vc, nsa_new, win_new, wbuf_t, gl, *([cache_t] * n_pages))


def _rope_tables(pos):
    half = HEAD_DIM // 2
    freq = ROPE_THETA ** (-jnp.arange(half, dtype=F32) / half)
    ang = pos.astype(F32)[:, None] * freq[None, :]
    cos, sin = jnp.cos(ang), jnp.sin(ang)
    return jnp.tile(cos, (1, LANES // half)), jnp.tile(jnp.concatenate([-sin, sin], axis=1), (1, LANES // HEAD_DIM))


def _trunk(groups, mod4, mod5, wts):
    (g_mix, g_ffn, g_final, w_in_even_t, w_gate_t, sconv_w, w_out_even, w_in_odd, cconv_w, cconv_b, c_ln_g,
     c_ln_b, d_ln_g, d_ln_b, d_ws, d_bs_t, w_out_odd, w_up, ffn_conv_w, w_down) = wts
    prompt, sample = groups
    depth = g_mix.shape[0]
    d = prompt["x"].shape[1]
    sp = prompt["seq"]
    a_w = sconv_w.shape[-1]
    dff = ffn_conv_w.shape[-1]
    kv_cols = 3 * a_w + N_HEADS * HEAD_DIM + 6 * KV_HEADS * HEAD_DIM
    mm_tm = 1024
    down_tm = 512
    out_tn = 1024

    def prompt_gate(l, which, tm, tn):
        return pl.BlockSpec((None, None, None, 1, tn), lambda j, i: (l, which, (i * tm) // sp, 0, j))

    def sample_gate(l, which):
        b0 = sample["b_off"]
        return jnp.repeat(mod4[l, which, b0:b0 + sample["batch"]], sample["seq"], axis=0)

    def norm(xs, g, l, which):
        return [_norm_mod(x, g, mod5, l, which, grp["b_off"], grp["tm"], grp["tpb"]) for x, grp in zip(xs, groups)]

    xs = [prompt["x"], sample["x"]]
    new = [dict(nsa=[], win=[], s=[], c=[], dv=[], f=[]) for _ in groups]
    for l in range(depth):
        i = l // 2
        h = norm(xs, g_mix, l, 0)
        if l % 2 == 0:
            zs = _mm([h[0]], w_in_even_t, i, kv_cols, mm_tm, kv_cols // 4, wt=True, second=([h[1]], None, None),
                     name="in_even")
            gls = _mm([h[0]], w_gate_t, i, KV_HEADS * LANES, mm_tm, 512, wt=True, second=([h[1]], None, None),
                      name="gate_logits")
            mixed = []
            for z, gl, grp, out in zip(zs, gls, groups, new):
                mix_a, sb = _mixer_a(z, sconv_w, i, grp["sconv"][i], grp["tm"], grp["tpb"])
                o_b, nsa_rows, win_state = grp["nsa"](i, z, gl)
                mixed.append([mix_a, o_b])
                out["s"].append(sb)
                out["nsa"].append(nsa_rows)
                out["win"].append(win_state)
            w_out = w_out_even
        else:
            zs = _mm([h[0]], w_in_odd, i, w_in_odd.shape[-1], mm_tm, out_tn, second=([h[1]], None, None),
                     name="in_odd")
            mixed = []
            for z, grp, out in zip(zs, groups, new):
                mix, cb, v = _odd_post(z, i, grp["cconv"][i], cconv_w, cconv_b, c_ln_g, c_ln_b, d_ln_g, d_ln_b,
                                       d_ws, d_bs_t, grp["tm"], grp["tpb"])
                mixed.append([mix])
                out["c"].append(cb)
                out["dv"].append(v)
            w_out = w_out_odd
        xs = list(_mm(mixed[0], w_out, i, d, mm_tm, out_tn, res=xs[0], gate_spec=prompt_gate(l, 2, mm_tm, out_tn),
                      gate=mod5, second=(mixed[1], xs[1], sample_gate(l, 2)), name="out_proj"))
        h = norm(xs, g_ffn, l, 3)
        act_p, fb_p, a_s, g_s = _ffn_up(h[0], w_up, ffn_conv_w, l, prompt["ffn"][l], mm_tm, sp // mm_tm, 512, h[1])
        act_s, fb_s = _ffn_act(a_s, g_s, ffn_conv_w, l, sample["ffn"][l], sample["tm"], sample["tpb"], dff)
        new[0]["f"].append(fb_p)
        new[1]["f"].append(fb_s)
        xs = list(_mm([act_p], w_down, l, d, down_tm, 512, res=xs[0], gate_spec=prompt_gate(l, 5, down_tm, 512),
                      gate=mod5, second=([act_s], xs[1], sample_gate(l, 5)), name="ffn_down"))
    ys = [_final_norm(x, g_final, grp["tm"]) for x, grp in zip(xs, groups)]
    return ys, new


def kernel(x_prompt, x_sample, cache_nsa_kv, state_win_kv, state_sconv, state_cconv, state_ffn_conv, page_table, c_prompt, c_sample, g_mix, g_ffn, g_final, w_ada, b_ada, w_in_even, sconv_w, w_out_even, w_in_odd, cconv_w, cconv_b, c_ln_g, c_ln_b, d_ln_g, d_ln_b, d_ws, d_bs, w_out_odd, w_up, ffn_conv_w, w_down):
    bp, sp, d = x_prompt.shape
    bs, ss, _ = x_sample.shape
    depth = g_mix.shape[0]
    n_even = w_in_even.shape[0]
    past = page_table.shape[1] * PAGE_SIZE
    a_w = sconv_w.shape[-1]
    dff = ffn_conv_w.shape[-1]
    kvw = KV_HEADS * HEAD_DIM
    dt = x_prompt.dtype

    rows = -(-(bp + bs) // 8) * 8
    c_all = jnp.concatenate([c_prompt, c_sample, jnp.zeros((rows - bp - bs, d), dt)], axis=0)
    mod4 = _ada(c_all, w_ada, b_ada)
    mod5 = mod4.reshape(depth, 6, rows, 1, d)

    gate_c0 = 3 * a_w + N_HEADS * HEAD_DIM + 6 * kvw
    w_in_even_t = jnp.swapaxes(w_in_even, 1, 2)
    wg = w_in_even_t[:, gate_c0:, :].reshape(n_even, KV_HEADS, GQA * 3, d)
    w_gate_t = jnp.pad(wg, ((0, 0), (0, 0), (0, LANES - GQA * 3), (0, 0))).reshape(n_even, KV_HEADS * LANES, d)

    wts = (g_mix, g_ffn, g_final, w_in_even_t, w_gate_t, sconv_w, w_out_even, w_in_odd, cconv_w, cconv_b, c_ln_g,
           c_ln_b, d_ln_g, d_ln_b, d_ws, jnp.swapaxes(d_bs, 1, 2), w_out_odd, w_up, ffn_conv_w, w_down)

    tm_p = 512
    tq = 256
    cos_p, sin_p = _rope_tables(jnp.arange(sp, dtype=jnp.int32))

    def prompt_nsa(i, z, gl):
        qt, nsat, wint, kv, ksa, kc, vc = _rope_prompt(z, cos_p, sin_p, bp, sp, tm_p)
        o_b = _nsa_prompt(qt, kc, vc, ksa, kv, gl, bp, sp, tq)
        keep = min(WINDOW, sp)
        nsa_rows = jnp.transpose(nsat, (0, 4, 1, 2, 3))
        win_state = jnp.transpose(wint[..., sp - keep:], (0, 4, 1, 2, 3))
        return o_b, nsa_rows, win_state

    zeros = lambda n, r, w: jnp.zeros((n, bp, r, w), dt)
    prompt = dict(x=x_prompt.reshape(bp * sp, d), b_off=0, batch=bp, seq=sp, tm=tm_p, tpb=sp // tm_p,
                  sconv=zeros(n_even, 2, a_w), cconv=zeros(depth // 2, CCONV_W - 1, a_w), ffn=zeros(depth, 2, dff),
                  nsa=prompt_nsa)

    cache_t = jnp.transpose(cache_nsa_kv, (0, 1, 3, 4, 5, 2))
    wbuf_t = jnp.transpose(state_win_kv, (0, 1, 3, 4, 5, 2))
    cos_s, sin_s = _rope_tables(past + jnp.arange(ss, dtype=jnp.int32))

    def sample_nsa(i, z, gl):
        q, nsa_new, win_new = _rope_sample(z, cos_s, sin_s, bs, ss)
        kc, vc = _cmp_means(cache_t, i, page_table, 16)
        o_b = _nsa_sample(cache_t, i, page_table, q, kc, vc, nsa_new, win_new, wbuf_t, gl, 32, past, ss)
        win_new_t = jnp.transpose(win_new.reshape(bs, ss, 2, KV_HEADS, HEAD_DIM), (0, 2, 3, 4, 1))
        win_t = jnp.concatenate([wbuf_t[i], win_new_t], axis=-1)[..., ss:]
        win_state = jnp.transpose(win_t, (0, 4, 1, 2, 3))
        return o_b, nsa_new.reshape(bs, ss, 4, KV_HEADS, HEAD_DIM), win_state

    sample = dict(x=x_sample.reshape(bs * ss, d), b_off=bp, batch=bs, seq=ss, tm=ss, tpb=1,
                  sconv=state_sconv, cconv=state_cconv, ffn=state_ffn_conv, nsa=sample_nsa)

    (y_p, y_s), (new_p, new_s) = _trunk((prompt, sample), mod4, mod5, wts)

    nsa_p = [a.reshape(bp, sp, 4, KV_HEADS, HEAD_DIM) for a in new_p["nsa"]]
    win_p = [a.reshape(bp, -1, 2, KV_HEADS, HEAD_DIM) for a in new_p["win"]]
    dv_s = [a.reshape(bs, ss, -1) for a in new_s["dv"]]
    return (y_p.reshape(bp, sp, d), y_s.reshape(bs, ss, d), jnp.stack(nsa_p), jnp.stack(new_s["nsa"]),
            jnp.stack(win_p), jnp.stack(new_s["win"]), jnp.stack(new_p["s"]), jnp.stack(new_s["s"]),
            jnp.stack(new_p["c"]), jnp.stack(new_s["c"]), jnp.stack(dv_s), jnp.stack(new_p["f"]),
            jnp.stack(new_s["f"]))
```

```python
import functools
import math

import jax
import jax.numpy as jnp
from jax import lax
from jax.experimental import pallas as pl
from jax.experimental.pallas import tpu as pltpu

BF = jnp.bfloat16
F32 = jnp.float32

HEAD_DIM = 64
N_HEADS = 16
KV_HEADS = 4
GQA = N_HEADS // KV_HEADS
NSA_BLOCK = 64
N_SEL = 16
WINDOW = 512
PAGE_SIZE = 128
ROPE_THETA = 10000.0
CCONV_W = 31
D_CHUNK = 128
D_GROUPS = 4
EPS = 1e-6
NEG = -1e30

Q_SCALE = HEAD_DIM ** -0.5 * math.log2(math.e)

LANES = 128
VMEM_LIMIT = 56 * 1024 * 1024


def _params(*sem):
    return pltpu.CompilerParams(dimension_semantics=sem, vmem_limit_bytes=VMEM_LIMIT)


def _sigmoid(x):
    return 1.0 / (1.0 + jnp.exp(-x))


def _silu(x):
    return x * _sigmoid(x)


def _gelu_tanh(x):
    return 0.5 * x * (1.0 + jnp.tanh(math.sqrt(2.0 / math.pi) * (x + 0.044715 * (x * x * x))))


def _layernorm(x, g, b):
    mu = jnp.mean(x, axis=-1, keepdims=True)
    xc = x - mu
    return xc * lax.rsqrt(jnp.mean(xc * xc, axis=-1, keepdims=True) + EPS) * g + b


def _ada_kernel(c_ref, w_ref, b_ref, o_ref):
    ca = _silu(c_ref[...]).astype(BF)
    acc = jnp.dot(ca, w_ref[...].astype(BF), preferred_element_type=F32)
    o_ref[...] = acc + b_ref[...]


def _ada(c16, w_ada, b_ada):
    depth, d, n6 = w_ada.shape
    rows = c16.shape[0]
    tn = 1024
    per = d // tn
    return pl.pallas_call(
        _ada_kernel,
        grid=(depth, n6 // tn),
        in_specs=[
            pl.BlockSpec((rows, d), lambda l, j: (0, 0)),
            pl.BlockSpec((None, d, tn), lambda l, j: (l, 0, j)),
            pl.BlockSpec((None, 1, tn), lambda l, j: (l, 0, j)),
        ],
        out_specs=pl.BlockSpec((None, None, rows, tn), lambda l, j: (l, j // per, 0, j % per)),
        out_shape=jax.ShapeDtypeStruct((depth, 6, rows, d), F32),
        compiler_params=_params("arbitrary", "arbitrary"),
        name="ada",
    )(c16, w_ada, b_ada.reshape(depth, 1, n6))


def _norm_mod_kernel(x_ref, g_ref, sh_ref, sc_ref, o_ref):
    x = x_ref[...]
    y = x * lax.rsqrt(jnp.mean(x * x, axis=-1, keepdims=True) + EPS) * g_ref[...]
    o_ref[...] = (y * (1.0 + sc_ref[...]) + sh_ref[...]).astype(o_ref.dtype)


def _norm_mod(x, g, mod5, l, which, b_off, tm, tpb):
    m, d = x.shape
    mod_spec = lambda w: pl.BlockSpec((None, None, None, 1, d),
                                      lambda i: (l, w, b_off + i // tpb, 0, 0))
    return pl.pallas_call(
        _norm_mod_kernel,
        grid=(m // tm,),
        in_specs=[
            pl.BlockSpec((tm, d), lambda i: (i, 0)),
            pl.BlockSpec((None, 1, d), lambda i: (l, 0, 0)),
            mod_spec(which), mod_spec(which + 1),
        ],
        out_specs=pl.BlockSpec((tm, d), lambda i: (i, 0)),
        out_shape=jax.ShapeDtypeStruct((m, d), BF),
        compiler_params=_params("arbitrary"),
        name="norm_mod",
    )(x, g.reshape(g.shape[0], 1, d), mod5, mod5)


def _final_norm_kernel(x_ref, g_ref, o_ref):
    x = x_ref[...]
    o_ref[...] = x * lax.rsqrt(jnp.mean(x * x, axis=-1, keepdims=True) + EPS) * g_ref[...]


def _final_norm(x, g, tm):
    m, d = x.shape
    return pl.pallas_call(
        _final_norm_kernel,
        grid=(m // tm,),
        in_specs=[pl.BlockSpec((tm, d), lambda i: (i, 0)), pl.BlockSpec((1, d), lambda i: (0, 0))],
        out_specs=pl.BlockSpec((tm, d), lambda i: (i, 0)),
        out_shape=jax.ShapeDtypeStruct((m, d), F32),
        compiler_params=_params("arbitrary"),
        name="final_norm",
    )(x, g.reshape(1, d))


def _mm_kernel(*refs, k_sizes, res_gate, wt, second):
    n_a = len(k_sizes)
    per = n_a + (2 if res_gate else 0)
    w_ref = refs[0]
    groups = [refs[1:1 + per]] + ([refs[1 + per:1 + 2 * per]] if second else [])
    pos = 1 + per * len(groups)
    o_refs = refs[pos:pos + len(groups)]
    wb_ref = refs[pos + len(groups)]

    @pl.when(pl.program_id(1) == 0)
    def _():
        wb_ref[...] = w_ref[...].astype(BF)

    def product(group, o_ref):
        acc = None
        k0 = 0
        for a_ref, ks in zip(group[:n_a], k_sizes):
            if wt:
                part = lax.dot_general(a_ref[...], wb_ref[:, k0:k0 + ks], (((1,), (1,)), ((), ())),
                                       preferred_element_type=F32)
            else:
                part = jnp.dot(a_ref[...], wb_ref[k0:k0 + ks, :], preferred_element_type=F32)
            acc = part if acc is None else acc + part
            k0 += ks
        if res_gate:
            acc = group[n_a][...] + group[n_a + 1][...] * acc
        o_ref[...] = acc.astype(o_ref.dtype)

    product(groups[0], o_refs[0])
    if second:
        @pl.when(pl.program_id(1) == pl.num_programs(1) - 1)
        def _():
            product(groups[1], o_refs[1])


def _mm(a_list, w, l, n, tm, tn, wt=False, res=None, gate_spec=None, gate=None, second=None, name="mm"):
    m = a_list[0].shape[0]
    k_sizes = tuple(a.shape[1] for a in a_list)
    k = sum(k_sizes)
    assert w.shape[2 if wt else 1] == k and n % tn == 0 and m % tm == 0
    if wt:
        in_specs = [pl.BlockSpec((None, tn, k), lambda j, i: (l, j, 0))]
    else:
        in_specs = [pl.BlockSpec((None, k, tn), lambda j, i: (l, 0, j))]
    args = [w]
    in_specs += [pl.BlockSpec((tm, ks), lambda j, i: (i, 0)) for ks in k_sizes]
    args += list(a_list)
    if res is not None:
        in_specs += [pl.BlockSpec((tm, tn), lambda j, i: (i, j)), gate_spec]
        args += [res, gate]
    out_specs = [pl.BlockSpec((tm, tn), lambda j, i: (i, j))]
    out_shape = [jax.ShapeDtypeStruct((m, n), F32)]
    if second is not None:
        a_list2, res2, gate2 = second
        m2 = a_list2[0].shape[0]
        in_specs += [pl.BlockSpec((m2, ks), lambda j, i: (0, 0)) for ks in k_sizes]
        args += list(a_list2)
        if res is not None:
            in_specs += [pl.BlockSpec((m2, tn), lambda j, i: (0, j))] * 2
            args += [res2, gate2]
        out_specs.append(pl.BlockSpec((m2, tn), lambda j, i: (0, j)))
        out_shape.append(jax.ShapeDtypeStruct((m2, n), F32))
    outs = pl.pallas_call(
        functools.partial(_mm_kernel, k_sizes=k_sizes, res_gate=res is not None, wt=wt, second=second is not None),
        grid=(n // tn, m // tm),
        in_specs=in_specs,
        out_specs=out_specs,
        out_shape=out_shape,
        scratch_shapes=[pltpu.VMEM((tn, k) if wt else (k, tn), BF)],
        compiler_params=_params("arbitrary", "arbitrary"),
        name=name,
    )(*args)
    return outs if second is not None else outs[0]


def _conv3(u, p, w):
    row = lax.broadcasted_iota(jnp.int32, u.shape, 0)
    um1 = jnp.where(row == 0, p[1:2], pltpu.roll(u, 1, 0))
    um2 = jnp.where(row == 0, p[0:1], jnp.where(row == 1, p[1:2], pltpu.roll(u, 2, 0)))
    return w[0:1] * um2 + w[1:2] * um1 + w[2:3] * u


def _prev_rows(i, tpb, prev_ref, carry_ref):
    if tpb == 1:
        return prev_ref[...]
    return jnp.where(i % tpb == 0, prev_ref[...], carry_ref[6:8, :])


def _mixer_a_kernel(ain_ref, ab_ref, ac_ref, w_ref, prev_ref, o_ref, st_ref, carry_ref, *, tm, tpb):
    i = pl.program_id(0)
    u = ac_ref[...] * ain_ref[...]
    p = _prev_rows(i, tpb, prev_ref, carry_ref)
    o_ref[...] = (ab_ref[...] * _conv3(u, p, w_ref[...])).astype(o_ref.dtype)
    st_ref[...] = u[tm - 2:tm]
    if tpb > 1:
        carry_ref[...] = u[tm - 8:tm]


def _mixer_a(z, sconv_w, l, prev, tm, tpb):
    m = z.shape[0]
    c = prev.shape[-1]
    nb = prev.shape[0]
    return pl.pallas_call(
        functools.partial(_mixer_a_kernel, tm=tm, tpb=tpb),
        grid=(m // tm,),
        in_specs=[
            pl.BlockSpec((tm, c), lambda i: (i, 0)),
            pl.BlockSpec((tm, c), lambda i: (i, 1)),
            pl.BlockSpec((tm, c), lambda i: (i, 2)),
            pl.BlockSpec((None, 3, c), lambda i: (l, 0, 0)),
            pl.BlockSpec((None, 2, c), lambda i: (i // tpb, 0, 0)),
        ],
        out_specs=[
            pl.BlockSpec((tm, c), lambda i: (i, 0)),
            pl.BlockSpec((None, 2, c), lambda i: (i // tpb, 0, 0)),
        ],
        out_shape=[jax.ShapeDtypeStruct((m, c), BF), jax.ShapeDtypeStruct((nb, 2, c), F32)],
        scratch_shapes=[pltpu.VMEM((8, c), F32)],
        compiler_params=_params("arbitrary"),
        name="mixer_a",
    )(z, z, z, sconv_w, prev)


def _ffn_act_kernel(a_ref, g_ref, w_ref, prev_ref, o_ref, st_ref, carry_ref, *, tm, tpb):
    i = pl.program_id(1)
    a = a_ref[...]
    p = _prev_rows(i, tpb, prev_ref, carry_ref)
    o_ref[...] = (_silu(_conv3(a, p, w_ref[...])) * g_ref[...]).astype(o_ref.dtype)
    st_ref[...] = a[tm - 2:tm]
    if tpb > 1:
        carry_ref[...] = a[tm - 8:tm]


def _ffn_act(za, zg, conv_w, l, prev, tm, tpb, tn):
    m = za.shape[0]
    dff = prev.shape[-1]
    nb = prev.shape[0]
    nj = dff // tn
    return pl.pallas_call(
        functools.partial(_ffn_act_kernel, tm=tm, tpb=tpb),
        grid=(nj, m // tm),
        in_specs=[
            pl.BlockSpec((tm, tn), lambda j, i: (i, j)),
            pl.BlockSpec((tm, tn), lambda j, i: (i, j)),
            pl.BlockSpec((None, 3, tn), lambda j, i: (l, 0, j)),
            pl.BlockSpec((None, 2, tn), lambda j, i: (i // tpb, 0, j)),
        ],
        out_specs=[
            pl.BlockSpec((tm, tn), lambda j, i: (i, j)),
            pl.BlockSpec((None, 2, tn), lambda j, i: (i // tpb, 0, j)),
        ],
        out_shape=[jax.ShapeDtypeStruct((m, dff), BF), jax.ShapeDtypeStruct((nb, 2, dff), F32)],
        scratch_shapes=[pltpu.VMEM((8, tn), F32)],
        compiler_params=_params("arbitrary", "arbitrary"),
        name="ffn_act",
    )(za, zg, conv_w, prev)


def _ffn_up_kernel(h_ref, wa_ref, wg_ref, cw_ref, prev_ref, h2_ref, o_ref, st_ref, a2_ref, g2_ref,
                   wab_ref, wgb_ref, carry_ref, *, tm, tpb):
    i = pl.program_id(1)

    @pl.when(i == 0)
    def _():
        wab_ref[...] = wa_ref[...].astype(BF)
        wgb_ref[...] = wg_ref[...].astype(BF)

    h = h_ref[...]
    a = jnp.dot(h, wab_ref[...], preferred_element_type=F32)
    g = jnp.dot(h, wgb_ref[...], preferred_element_type=F32)
    p = _prev_rows(i, tpb, prev_ref, carry_ref)
    o_ref[...] = (_silu(_conv3(a, p, cw_ref[...])) * g).astype(o_ref.dtype)
    st_ref[...] = a[tm - 2:tm]
    if tpb > 1:
        carry_ref[...] = a[tm - 8:tm]

    @pl.when(i == pl.num_programs(1) - 1)
    def _():
        a2_ref[...] = jnp.dot(h2_ref[...], wab_ref[...], preferred_element_type=F32)
        g2_ref[...] = jnp.dot(h2_ref[...], wgb_ref[...], preferred_element_type=F32)


def _ffn_up(h, w_up, conv_w, l, prev, tm, tpb, tn, h2):
    m, d = h.shape
    m2 = h2.shape[0]
    dff = prev.shape[-1]
    nb = prev.shape[0]
    nj = dff // tn
    small = pl.BlockSpec((m2, tn), lambda j, i: (0, j))
    return pl.pallas_call(
        functools.partial(_ffn_up_kernel, tm=tm, tpb=tpb),
        grid=(nj, m // tm),
        in_specs=[
            pl.BlockSpec((tm, d), lambda j, i: (i, 0)),
            pl.BlockSpec((None, d, tn), lambda j, i: (l, 0, j)),
            pl.BlockSpec((None, d, tn), lambda j, i: (l, 0, j + nj)),
            pl.BlockSpec((None, 3, tn), lambda j, i: (l, 0, j)),
            pl.BlockSpec((None, 2, tn), lambda j, i: (i // tpb, 0, j)),
            pl.BlockSpec((m2, d), lambda j, i: (0, 0)),
        ],
        out_specs=[
            pl.BlockSpec((tm, tn), lambda j, i: (i, j)),
            pl.BlockSpec((None, 2, tn), lambda j, i: (i // tpb, 0, j)),
            small, small,
        ],
        out_shape=[jax.ShapeDtypeStruct((m, dff), BF), jax.ShapeDtypeStruct((nb, 2, dff), F32),
                   jax.ShapeDtypeStruct((m2, dff), F32), jax.ShapeDtypeStruct((m2, dff), F32)],
        scratch_shapes=[pltpu.VMEM((d, tn), BF), pltpu.VMEM((d, tn), BF), pltpu.VMEM((8, tn), F32)],
        compiler_params=_params("arbitrary", "arbitrary"),
        name="ffn_up_fused",
    )(h, w_up, w_up, conv_w, prev, h2)


def _odd_post_kernel(ca_ref, cg_ref, du_ref, dv_ref, cw_ref, cb_ref, clg_ref, clb_ref, dlg_ref,
                     dlb_ref, ws_ref, bst_ref, prev_ref, o_ref, st_ref, v_ref, cbuf, wbuf, *, tm, tpb):
    i = pl.program_id(0)
    cw = cw_ref.shape[-1]
    c = ca_ref[...] * _sigmoid(cg_ref[...])
    if tpb == 1:
        cbuf[2:32, :] = prev_ref[...]
    else:
        @pl.when(i % tpb == 0)
        def _():
            cbuf[2:32, :] = prev_ref[...]

        @pl.when(i % tpb != 0)
        def _():
            cbuf[0:32, :] = cbuf[tm:tm + 32, :]
    cbuf[32:32 + tm, :] = c
    acc = None
    for r in range(8):
        taps = range(r, CCONV_W, 8)
        rows = tm + 8 * (len(taps) - 1)
        wbuf[0:rows, :] = cbuf[2 + r:2 + r + rows, :]
        for j, k in enumerate(taps):
            term = cw_ref[k:k + 1, :] * wbuf[8 * j:8 * j + tm, :]
            acc = term if acc is None else acc + term
    st_ref[...] = cbuf[tm + 2:tm + 32, :]
    o_ref[:, 0:cw] = _silu(_layernorm(acc + cb_ref[...], clg_ref[...], clb_ref[...])).astype(o_ref.dtype)

    u = _gelu_tanh(du_ref[...])
    v = _layernorm(_gelu_tanh(dv_ref[...]), dlg_ref[...], dlb_ref[...])
    v_ref[...] = v
    gw = cw // D_GROUPS
    trow = lax.broadcasted_iota(jnp.int32, (D_CHUNK, D_CHUNK), 0)
    tcol = lax.broadcasted_iota(jnp.int32, (D_CHUNK, D_CHUNK), 1)
    rows = min(tm, D_CHUNK)
    for ch in range(max(1, tm // D_CHUNK)):
        r0 = ch * D_CHUNK
        vch = v[r0:r0 + rows]
        if rows < D_CHUNK:
            vch = jnp.concatenate([vch, jnp.zeros((D_CHUNK - rows, cw), F32)], axis=0)
        vch = vch.astype(BF)
        for g in range(D_GROUPS):
            wg = jnp.where(tcol <= trow, ws_ref[g], 0.0).astype(BF)
            zz = jnp.dot(wg, vch[:, g * gw:(g + 1) * gw], preferred_element_type=F32)
            zz = zz + bst_ref[:, g:g + 1]
            o_ref[r0:r0 + rows, cw + g * gw:cw + (g + 1) * gw] = (
                u[r0:r0 + rows, g * gw:(g + 1) * gw] * zz[0:rows]).astype(o_ref.dtype)


def _odd_post(z, i_odd, prev, cconv_w, cconv_b, c_ln_g, c_ln_b, d_ln_g, d_ln_b, d_ws, d_bs_t, tm, tpb):
    m = z.shape[0]
    c = prev.shape[-1]
    nb = prev.shape[0]
    vec = lambda: pl.BlockSpec((None, 1, c), lambda i: (i_odd, 0, 0))
    r3 = lambda a: a.reshape(a.shape[0], 1, c)
    return pl.pallas_call(
        functools.partial(_odd_post_kernel, tm=tm, tpb=tpb),
        grid=(m // tm,),
        in_specs=[
            pl.BlockSpec((tm, c), lambda i: (i, 0)),
            pl.BlockSpec((tm, c), lambda i: (i, 1)),
            pl.BlockSpec((tm, c), lambda i: (i, 2)),
            pl.BlockSpec((tm, c), lambda i: (i, 3)),
            pl.BlockSpec((None, CCONV_W, c), lambda i: (i_odd, 0, 0)),
            vec(), vec(), vec(), vec(), vec(),
            pl.BlockSpec((None, D_GROUPS, D_CHUNK, D_CHUNK), lambda i: (i_odd, 0, 0, 0)),
            pl.BlockSpec((None, D_CHUNK, D_GROUPS), lambda i: (i_odd, 0, 0)),
            pl.BlockSpec((None, CCONV_W - 1, c), lambda i: (i // tpb, 0, 0)),
        ],
        out_specs=[
            pl.BlockSpec((tm, 2 * c), lambda i: (i, 0)),
            pl.BlockSpec((None, CCONV_W - 1, c), lambda i: (i // tpb, 0, 0)),
            pl.BlockSpec((tm, c), lambda i: (i, 0)),
        ],
        out_shape=[jax.ShapeDtypeStruct((m, 2 * c), BF),
                   jax.ShapeDtypeStruct((nb, CCONV_W - 1, c), F32),
                   jax.ShapeDtypeStruct((m, c), F32)],
        scratch_shapes=[pltpu.VMEM((32 + tm, c), F32), pltpu.VMEM((24 + tm, c), F32)],
        compiler_params=_params("arbitrary"),
        name="odd_post",
    )(z, z, z, z, cconv_w, r3(cconv_b), r3(c_ln_g), r3(c_ln_b), r3(d_ln_g), r3(d_ln_b), d_ws, d_bs_t, prev)


def _rope128(x, cos, sin_signed):
    lane = lax.broadcasted_iota(jnp.int32, x.shape, 1)
    swapped = jnp.where((lane & (HEAD_DIM - 1)) < HEAD_DIM // 2,
                        pltpu.roll(x, LANES - HEAD_DIM // 2, 1), pltpu.roll(x, HEAD_DIM // 2, 1))
    return x * cos + swapped * sin_signed


def _rope_slab(ref, c0, width, cos, sin_signed):
    return jnp.concatenate(
        [_rope128(ref[:, c0 + k * LANES:c0 + (k + 1) * LANES], cos, sin_signed) for k in range(width // LANES)],
        axis=1)


def _rope_prompt_kernel(zq_ref, zn_ref, zw_ref, cos_ref, sin_ref, qt_ref, nsat_ref, wint_ref, kv_ref, ksa_ref,
                        kc_ref, vc_ref, *, ts, tpb):
    cos, sin = cos_ref[...], sin_ref[...]
    kvw = KV_HEADS * HEAD_DIM
    scale = Q_SCALE
    for k in range(N_HEADS * HEAD_DIM // LANES):
        rt = (_rope128(zq_ref[:, k * LANES:(k + 1) * LANES], cos, sin) * scale).T
        qt_ref[2 * k] = rt[:HEAD_DIM].astype(qt_ref.dtype)
        qt_ref[2 * k + 1] = rt[HEAD_DIM:].astype(qt_ref.dtype)
    k_cmp = _rope_slab(zn_ref, 0, kvw, cos, sin)
    v_cmp = zn_ref[:, kvw:2 * kvw]
    k_sel = _rope_slab(zn_ref, 2 * kvw, kvw, cos, sin)
    v_sel = zn_ref[:, 3 * kvw:4 * kvw]
    k_win = _rope_slab(zw_ref, 0, kvw, cos, sin)
    v_win = zw_ref[:, kvw:2 * kvw]
    for t, slab in enumerate((k_cmp, v_cmp, k_sel, v_sel)):
        nsat_ref[t] = slab.T.reshape(KV_HEADS, HEAD_DIM, ts)
    for t, slab in enumerate((k_win, v_win)):
        wint_ref[t] = slab.T.reshape(KV_HEADS, HEAD_DIM, ts)
    for t, slab in enumerate((v_sel, k_win, v_win)):
        for g in range(KV_HEADS):
            kv_ref[t, g] = slab[:, g * HEAD_DIM:(g + 1) * HEAD_DIM].astype(kv_ref.dtype)
    row0 = (pl.program_id(0) % tpb) * ts
    blk_of_row = (row0 + lax.broadcasted_iota(jnp.int32, (ts, HEAD_DIM), 0)) // NSA_BLOCK
    onehot = jnp.where(blk_of_row == lax.broadcasted_iota(jnp.int32, (ts, HEAD_DIM), 1), 1.0, 0.0)
    for g in range(KV_HEADS):
        ksa_ref[g] = jnp.concatenate([k_sel[:, g * HEAD_DIM:(g + 1) * HEAD_DIM], onehot], axis=1).astype(ksa_ref.dtype)
    nblk = ts // NSA_BLOCK
    kc = jnp.sum(k_cmp.reshape(nblk, NSA_BLOCK, kvw), axis=1) * (1.0 / NSA_BLOCK)
    vc = jnp.sum(v_cmp.reshape(nblk, NSA_BLOCK, kvw), axis=1) * (1.0 / NSA_BLOCK)
    for g in range(KV_HEADS):
        kc_ref[g] = kc[:, g * HEAD_DIM:(g + 1) * HEAD_DIM]
        vc_ref[g] = vc[:, g * HEAD_DIM:(g + 1) * HEAD_DIM]


def _rope_prompt(z, cos, sin, batch, seq, ts):
    tpb = seq // ts
    qw = N_HEADS * HEAD_DIM
    nblk = ts // NSA_BLOCK
    assert seq // NSA_BLOCK <= HEAD_DIM
    return pl.pallas_call(
        functools.partial(_rope_prompt_kernel, ts=ts, tpb=tpb),
        grid=(batch * tpb,),
        in_specs=[
            pl.BlockSpec((ts, qw), lambda i: (i, 3)),
            pl.BlockSpec((ts, qw), lambda i: (i, 4)),
            pl.BlockSpec((ts, qw // 2), lambda i: (i, 10)),
            pl.BlockSpec((ts, LANES), lambda i: (i % tpb, 0)),
            pl.BlockSpec((ts, LANES), lambda i: (i % tpb, 0)),
        ],
        out_specs=[
            pl.BlockSpec((None, N_HEADS, HEAD_DIM, ts), lambda i: (i // tpb, 0, 0, i % tpb)),
            pl.BlockSpec((None, 4, KV_HEADS, HEAD_DIM, ts), lambda i: (i // tpb, 0, 0, 0, i % tpb)),
            pl.BlockSpec((None, 2, KV_HEADS, HEAD_DIM, ts), lambda i: (i // tpb, 0, 0, 0, i % tpb)),
            pl.BlockSpec((None, 3, KV_HEADS, ts, HEAD_DIM), lambda i: (i // tpb, 0, 0, i % tpb, 0)),
            pl.BlockSpec((None, KV_HEADS, ts, 2 * HEAD_DIM), lambda i: (i // tpb, 0, i % tpb, 0)),
            pl.BlockSpec((None, KV_HEADS, nblk, HEAD_DIM), lambda i: (i // tpb, 0, i % tpb, 0)),
            pl.BlockSpec((None, KV_HEADS, nblk, HEAD_DIM), lambda i: (i // tpb, 0, i % tpb, 0)),
        ],
        out_shape=[
            jax.ShapeDtypeStruct((batch, N_HEADS, HEAD_DIM, seq), BF),
            jax.ShapeDtypeStruct((batch, 4, KV_HEADS, HEAD_DIM, seq), F32),
            jax.ShapeDtypeStruct((batch, 2, KV_HEADS, HEAD_DIM, seq), F32),
            jax.ShapeDtypeStruct((batch, 3, KV_HEADS, seq, HEAD_DIM), BF),
            jax.ShapeDtypeStruct((batch, KV_HEADS, seq, 2 * HEAD_DIM), BF),
            jax.ShapeDtypeStruct((batch, KV_HEADS, seq // NSA_BLOCK, HEAD_DIM), F32),
            jax.ShapeDtypeStruct((batch, KV_HEADS, seq // NSA_BLOCK, HEAD_DIM), F32),
        ],
        compiler_params=_params("arbitrary"),
        name="rope_prompt",
    )(z, z, z, cos, sin)


def _rope_sample_kernel(zq_ref, zn_ref, zw_ref, cos_ref, sin_ref, q_ref, nsa_ref, win_ref):
    cos, sin = cos_ref[...], sin_ref[...]
    kvw = KV_HEADS * HEAD_DIM
    scale = Q_SCALE
    for k in range(N_HEADS * HEAD_DIM // LANES):
        r = _rope128(zq_ref[:, k * LANES:(k + 1) * LANES], cos, sin) * scale
        q_ref[2 * k] = r[:, :HEAD_DIM]
        q_ref[2 * k + 1] = r[:, HEAD_DIM:]
    nsa_ref[:, 0:kvw] = _rope_slab(zn_ref, 0, kvw, cos, sin)
    nsa_ref[:, kvw:2 * kvw] = zn_ref[:, kvw:2 * kvw]
    nsa_ref[:, 2 * kvw:3 * kvw] = _rope_slab(zn_ref, 2 * kvw, kvw, cos, sin)
    nsa_ref[:, 3 * kvw:4 * kvw] = zn_ref[:, 3 * kvw:4 * kvw]
    win_ref[:, 0:kvw] = _rope_slab(zw_ref, 0, kvw, cos, sin)
    win_ref[:, kvw:2 * kvw] = zw_ref[:, kvw:2 * kvw]


def _rope_sample(z, cos, sin, batch, seq):
    qw = N_HEADS * HEAD_DIM
    return pl.pallas_call(
        _rope_sample_kernel,
        grid=(batch,),
        in_specs=[
            pl.BlockSpec((seq, qw), lambda i: (i, 3)),
            pl.BlockSpec((seq, qw), lambda i: (i, 4)),
            pl.BlockSpec((seq, qw // 2), lambda i: (i, 10)),
            pl.BlockSpec((seq, LANES), lambda i: (0, 0)),
            pl.BlockSpec((seq, LANES), lambda i: (0, 0)),
        ],
        out_specs=[
            pl.BlockSpec((None, N_HEADS, seq, HEAD_DIM), lambda i: (i, 0, 0, 0)),
            pl.BlockSpec((seq, qw), lambda i: (i, 0)),
            pl.BlockSpec((seq, qw // 2), lambda i: (i, 0)),
        ],
        out_shape=[
            jax.ShapeDtypeStruct((batch, N_HEADS, seq, HEAD_DIM), F32),
            jax.ShapeDtypeStruct((batch * seq, qw), F32),
            jax.ShapeDtypeStruct((batch * seq, qw // 2), F32),
        ],
        compiler_params=_params("arbitrary"),
        name="rope_sample",
    )(z, z, z, cos, sin)


def _cmp_branch(kc, vc, q, qpos, nq):
    nb = kc.shape[0]
    st = lax.dot_general(kc.astype(BF), q, (((1,), (1,)), ((), ())), preferred_element_type=F32)
    blk = lax.broadcasted_iota(jnp.int32, st.shape, 0)
    ok = (blk + 1) * NSA_BLOCK <= qpos + 1
    sm = jnp.where(ok, st, NEG)
    mx = jnp.max(sm, axis=0, keepdims=True)
    e = jnp.where(ok, jnp.exp2(sm - mx), 0.0)
    den = jnp.sum(e, axis=0, keepdims=True)
    pt = e / jnp.where(den > 0.0, den, 1.0)
    o_cmp = lax.dot_general(pt.astype(BF), vc.astype(BF), (((0,), (0,)), ((), ())), preferred_element_type=F32)
    imp = pt[:, 0:nq]
    for r in range(1, GQA):
        imp = imp + pt[:, r * nq:(r + 1) * nq]
    return o_cmp, imp


def _importance(imp, qpos_q):
    blk = lax.broadcasted_iota(jnp.int32, imp.shape, 0)
    cur = qpos_q // NSA_BLOCK
    forced = (blk == 0) | (blk == cur) | (blk == cur - 1)
    imp = jnp.where(forced, GQA + 1.0, imp)
    return jnp.where(blk <= cur, imp, -1.0)


def _select_topk(imp_ref, nb):
    imp = imp_ref[...]
    blk = lax.broadcasted_iota(jnp.int32, imp.shape, 0)

    def body(i, rank):
        row = imp_ref[pl.ds(i, 1), :]
        ahead = (row > imp) | ((row == imp) & (i < blk))
        return rank + jnp.where(ahead, 1.0, 0.0)

    rank = lax.fori_loop(0, nb, body, jnp.zeros(imp.shape, F32), unroll=8)
    return jnp.where(rank < float(N_SEL), 1.0, 0.0)


def _select_topk_packed(imp_ref, nb, width):
    groups = LANES // width
    per = nb // groups
    imp = imp_ref[...]
    blk = lax.broadcasted_iota(jnp.int32, imp.shape, 0)
    grp = lax.broadcasted_iota(jnp.int32, (1, LANES), 1) // width
    first = grp * per

    def body(i, rank):
        row = imp_ref[pl.ds(i, 1), :]
        for g in range(1, groups):
            row = jnp.where(grp == g, imp_ref[pl.ds(i + g * per, 1), :], row)
        ahead = (row > imp) | ((row == imp) & (i + first < blk))
        return rank + jnp.where(ahead, 1.0, 0.0)

    rank = lax.fori_loop(0, per, body, jnp.zeros(imp.shape, F32), unroll=6)
    total = rank[:, 0:width]
    for g in range(1, groups):
        total = total + rank[:, g * width:(g + 1) * width]
    return jnp.where(total < float(N_SEL), 1.0, 0.0)


def _expand_blocks(sel_t, n_keys, first_block):
    nbl = sel_t.shape[0]
    kb = lax.broadcasted_iota(jnp.int32, (nbl, n_keys), 1) // NSA_BLOCK + first_block
    nn = lax.broadcasted_iota(jnp.int32, (nbl, n_keys), 0)
    e = jnp.where(kb == nn, 1.0, 0.0).astype(BF)
    return lax.dot_general(sel_t.astype(BF), e, (((0,), (0,)), ((), ())), preferred_element_type=F32)


def _online_update(s, valid, vt, m_ref, l_ref, acc_ref, idx, v_rows=None):
    nk = s.shape[1]
    s = jnp.where(valid, s, NEG)
    m_prev = m_ref[idx]
    m_new = jnp.maximum(m_prev, jnp.max(s, axis=1, keepdims=True))
    alpha = jnp.exp2(m_prev - m_new)
    p = jnp.where(valid, jnp.exp2(s - jnp.concatenate([m_new] * (nk // LANES), axis=1)), 0.0)
    l_ref[idx] = alpha * l_ref[idx] + jnp.sum(p, axis=1, keepdims=True)
    if v_rows is None:
        pv = lax.dot_general(p.astype(BF), vt, (((1,), (1,)), ((), ())), preferred_element_type=F32)
    else:
        pv = jnp.dot(p.astype(BF), v_rows, preferred_element_type=F32)
    acc_ref[idx] = alpha[:, :HEAD_DIM] * acc_ref[idx] + pv
    m_ref[idx] = m_new


def _online_update_t(k_rows, qt, bias, v_rows, m_ref, l_ref, acc_ref, idx):
    st = jnp.dot(k_rows, qt, preferred_element_type=F32)
    nq = qt.shape[1] // GQA
    ps, alphas = [], []
    for r in range(GQA):
        cols = slice(r * nq, (r + 1) * nq)
        s_r = st[:, cols] if bias is None else st[:, cols] + bias
        m_prev = m_ref[idx, :, cols]
        m_new = jnp.maximum(m_prev, jnp.max(s_r, axis=0, keepdims=True))
        alpha = jnp.exp2(m_prev - m_new)
        p_r = jnp.exp2(s_r - m_new)
        l_ref[idx, :, cols] = alpha * l_ref[idx, :, cols] + jnp.sum(p_r, axis=0, keepdims=True)
        m_ref[idx, :, cols] = m_new
        ps.append(p_r.astype(BF))
        alphas.append(alpha)
    pv = lax.dot_general(v_rows, jnp.concatenate(ps, axis=1), (((0,), (0,)), ((), ())),
                         preferred_element_type=F32)
    acc_ref[idx] = jnp.concatenate(alphas, axis=1) * acc_ref[idx] + pv


def _nsa_prompt_kernel(qt_ref, kc_ref, vc_ref, ks_ref, vs_ref, kw_ref, vw_ref, gl_ref, o_ref,
                       imp_ref, m_ref, l_ref, acc_ref, *, tq, seq):
    qi = pl.program_id(2)
    q0 = qi * tq
    rq = GQA * tq
    nb = seq // NSA_BLOCK
    bpt = tq // NSA_BLOCK
    qt = jnp.concatenate([qt_ref[r] for r in range(GQA)], axis=1)

    st = jnp.dot(kc_ref[...].astype(BF), qt, preferred_element_type=F32)
    blk = lax.broadcasted_iota(jnp.int32, (nb, rq), 0)
    qpos_r = q0 + (lax.broadcasted_iota(jnp.int32, (nb, rq), 1) & (tq - 1))
    ok = (blk + 1) * NSA_BLOCK <= qpos_r + 1
    sm = jnp.where(ok, st, NEG)
    e = jnp.where(ok, jnp.exp2(sm - jnp.max(sm, axis=0, keepdims=True)), 0.0)
    den = jnp.sum(e, axis=0, keepdims=True)
    pt = e / jnp.where(den > 0.0, den, 1.0)
    o_cmp = lax.dot_general(vc_ref[...].astype(BF), pt.astype(BF), (((0,), (0,)), ((), ())),
                            preferred_element_type=F32)
    imp = pt[:, 0:tq]
    for r in range(1, GQA):
        imp = imp + pt[:, r * tq:(r + 1) * tq]
    imp_ref[...] = _importance(imp, q0 + lax.broadcasted_iota(jnp.int32, (nb, tq), 1))
    sel_bias = ((_select_topk(imp_ref, nb) - 1.0) * (-NEG)).astype(BF)
    q_sel = jnp.concatenate([qt, jnp.concatenate([sel_bias] * GQA, axis=1), jnp.zeros((HEAD_DIM - nb, rq), BF)],
                            axis=0)

    m_ref[...] = jnp.full(m_ref.shape, NEG, F32)
    l_ref[...] = jnp.zeros(l_ref.shape, F32)
    acc_ref[...] = jnp.zeros(acc_ref.shape, F32)

    krow = lax.broadcasted_iota(jnp.int32, (tq, tq), 0)
    qcol = lax.broadcasted_iota(jnp.int32, (tq, tq), 1)

    def rows_of(c):
        return pl.ds(pl.multiple_of(c * tq, tq), tq)

    def sel_chunk(c):
        _online_update_t(ks_ref[rows_of(c), :], q_sel, None, vs_ref[rows_of(c), :], m_ref, l_ref, acc_ref, 0)

    def sel_pair(j, carry):
        sel_chunk(2 * j)
        sel_chunk(2 * j + 1)
        return carry

    n_full = (WINDOW - tq) // tq
    n_far = jnp.maximum(qi - (n_full + 1), 0)
    lax.fori_loop(0, n_far // 2, sel_pair, 0)

    @pl.when(n_far % 2 == 1)
    def _():
        sel_chunk(n_far - 1)
    for rel in range(n_full + 1, 0, -1):
        @pl.when(qi >= rel)
        def _(rel=rel):
            c = qi - rel
            sel_chunk(c)
            bias = None if rel <= n_full else jnp.where(krow > qcol + (rel * tq - WINDOW), 0.0, NEG)
            _online_update_t(kw_ref[rows_of(c), :], qt, bias, vw_ref[rows_of(c), :], m_ref, l_ref, acc_ref, 1)
    causal_bias = jnp.where(krow <= qcol, 0.0, NEG)
    _online_update_t(ks_ref[rows_of(qi), :], q_sel, causal_bias, vs_ref[rows_of(qi), :], m_ref, l_ref, acc_ref, 0)
    _online_update_t(kw_ref[rows_of(qi), :], qt, causal_bias, vw_ref[rows_of(qi), :], m_ref, l_ref, acc_ref, 1)

    o_sel = acc_ref[0] * (1.0 / l_ref[0])
    o_win = acc_ref[1] * (1.0 / l_ref[1])
    gate_t = _sigmoid(gl_ref[...]).T
    outs = []
    for r in range(GQA):
        cols = slice(r * tq, (r + 1) * tq)
        o_t = (gate_t[3 * r:3 * r + 1] * o_cmp[:, cols] + gate_t[3 * r + 1:3 * r + 2] * o_sel[:, cols]
               + gate_t[3 * r + 2:3 * r + 3] * o_win[:, cols])
        outs.append(o_t.T)
    o_ref[...] = jnp.concatenate(outs, axis=1).astype(o_ref.dtype)


def _nsa_prompt(qt, kc, vc, ksa, kv, gl, batch, seq, tq):
    nq = seq // tq
    nb = seq // NSA_BLOCK
    rq = GQA * tq
    assert WINDOW % tq == 0
    kv_spec = lambda t: pl.BlockSpec((None, None, None, seq, HEAD_DIM), lambda b, g, i: (b, t, g, 0, 0))
    cmp_spec = pl.BlockSpec((None, None, nb, HEAD_DIM), lambda b, g, i: (b, g, 0, 0))
    return pl.pallas_call(
        functools.partial(_nsa_prompt_kernel, tq=tq, seq=seq),
        grid=(batch, KV_HEADS, nq),
        in_specs=[
            pl.BlockSpec((None, GQA, HEAD_DIM, tq), lambda b, g, i: (b, g, 0, i)),
            cmp_spec, cmp_spec,
            pl.BlockSpec((None, None, seq, 2 * HEAD_DIM), lambda b, g, i: (b, g, 0, 0)),
            kv_spec(0), kv_spec(1), kv_spec(2),
            pl.BlockSpec((tq, LANES), lambda b, g, i: (b * nq + i, g)),
        ],
        out_specs=pl.BlockSpec((tq, GQA * HEAD_DIM), lambda b, g, i: (b * nq + i, g)),
        out_shape=jax.ShapeDtypeStruct((batch * seq, N_HEADS * HEAD_DIM), BF),
        scratch_shapes=[
            pltpu.VMEM((nb, tq), F32),
            pltpu.VMEM((2, 1, rq), F32),
            pltpu.VMEM((2, 1, rq), F32),
            pltpu.VMEM((2, HEAD_DIM, rq), F32),
        ],
        compiler_params=_params("arbitrary", "arbitrary", "arbitrary"),
        name="nsa_prompt",
    )(qt, kc, vc, ksa, kv, kv, kv, gl)


def _cmp_means_kernel(pt_ref, *refs, n_pages):
    page_refs = refs[:n_pages]
    kc_ref, vc_ref = refs[n_pages], refs[n_pages + 1]
    bpp = PAGE_SIZE // NSA_BLOCK
    kvw = KV_HEADS * HEAD_DIM
    for t, out in enumerate((kc_ref, vc_ref)):
        x = jnp.concatenate([page_refs[p][t].reshape(kvw, PAGE_SIZE).T for p in range(n_pages)], axis=0)
        out[...] = jnp.sum(x.reshape(n_pages * bpp, NSA_BLOCK, kvw), axis=1) * (1.0 / NSA_BLOCK)


def _cmp_means(cache_t, layer, page_table, n_pages):
    batch, ppb = page_table.shape
    kvw = KV_HEADS * HEAD_DIM
    bpp = PAGE_SIZE // NSA_BLOCK
    steps = ppb // n_pages

    def page_spec(p):
        return pl.BlockSpec((None, None, 2, KV_HEADS, HEAD_DIM, PAGE_SIZE),
                            lambda b, s, pt: (layer, pt[b, s * n_pages + p], 0, 0, 0, 0))

    out_spec = pl.BlockSpec((None, n_pages * bpp, kvw), lambda b, s, pt: (b, s, 0))
    return pl.pallas_call(
        functools.partial(_cmp_means_kernel, n_pages=n_pages),
        grid_spec=pltpu.PrefetchScalarGridSpec(
            num_scalar_prefetch=1, grid=(batch, steps),
            in_specs=[page_spec(p) for p in range(n_pages)],
            out_specs=[out_spec, out_spec]),
        out_shape=[jax.ShapeDtypeStruct((batch, ppb * bpp, kvw), F32)] * 2,
        compiler_params=_params("arbitrary", "arbitrary"),
        name="cmp_means",
    )(page_table, *([cache_t] * n_pages))


def _nsa_sample_kernel(pt_ref, *refs, n_pages, past, s_new):
    (q_ref, kc_ref, vc_ref, new_ref, wnew_ref, wbuf_ref, gl_ref) = refs[:7]
    page_refs = refs[7:7 + n_pages]
    o_ref = refs[7 + n_pages]
    kcf, vcf, imp_ref, sel_ref, m_ref, l_ref, acc_ref, ocmp_ref = refs[8 + n_pages:]
    step = pl.program_id(1)
    nsteps = pl.num_programs(1)
    kvw = KV_HEADS * HEAD_DIM
    rq = GQA * s_new
    nbp = past // NSA_BLOCK
    nbf = kcf.shape[0]
    wlen = wbuf_ref.shape[-1]
    pad_rows = LANES - s_new

    def q_of(g):
        return q_ref[g * GQA:(g + 1) * GQA].reshape(rq, HEAD_DIM).astype(BF)

    def pad_keys(x):
        return jnp.concatenate([x, jnp.zeros((pad_rows, HEAD_DIM), F32)], axis=0).astype(BF)

    @pl.when(step == 0)
    def _():
        row8 = lax.broadcasted_iota(jnp.int32, (nbf - nbp, kvw), 0)
        for full, src, c0 in ((kcf, kc_ref, 0), (vcf, vc_ref, kvw)):
            full[0:nbp, :] = src[...]
            mean_new = jnp.sum(new_ref[:, c0:c0 + kvw], axis=0, keepdims=True) * (1.0 / NSA_BLOCK)
            full[nbp:nbf, :] = jnp.where(row8 == 0, mean_new, 0.0)
        col = lax.broadcasted_iota(jnp.int32, (nbf, rq), 1)
        qpos_q = past + lax.broadcasted_iota(jnp.int32, (nbf, s_new), 1)
        imps = []
        for g in range(KV_HEADS):
            lanes = slice(g * HEAD_DIM, (g + 1) * HEAD_DIM)
            o_cmp, imp = _cmp_branch(kcf[:, lanes], vcf[:, lanes], q_of(g), past + (col & (s_new - 1)), s_new)
            ocmp_ref[g] = o_cmp
            imps.append(_importance(imp, qpos_q))
        width = KV_HEADS * s_new
        imp_ref[...] = jnp.concatenate(imps * (LANES // width), axis=1)
        sel = _select_topk_packed(imp_ref, nbf, width)
        for g in range(KV_HEADS):
            sel_ref[g] = jnp.concatenate([sel[:, g * s_new:(g + 1) * s_new]] * GQA, axis=1)
        m_ref[...] = jnp.full(m_ref.shape, NEG, F32)
        l_ref[...] = jnp.zeros(l_ref.shape, F32)
        acc_ref[...] = jnp.zeros(acc_ref.shape, F32)

    nk = n_pages * PAGE_SIZE
    nbl = nk // NSA_BLOCK
    for g in range(KV_HEADS):
        kt = jnp.concatenate([page_refs[p][0, g] for p in range(n_pages)], axis=1).astype(BF)
        vt = jnp.concatenate([page_refs[p][1, g] for p in range(n_pages)], axis=1).astype(BF)
        sel_rows = sel_ref[g, pl.ds(pl.multiple_of(step * nbl, nbl), nbl), :]
        valid = _expand_blocks(sel_rows, nk, 0) > 0.5
        s = jnp.dot(q_of(g), kt, preferred_element_type=F32)
        _online_update(s, valid, vt, m_ref, l_ref, acc_ref, g)

    @pl.when(step == nsteps - 1)
    def _():
        tq_col = lax.broadcasted_iota(jnp.int32, (rq, LANES), 0) & (s_new - 1)
        tk = lax.broadcasted_iota(jnp.int32, (rq, LANES), 1)
        new_ok = (tk <= tq_col) & (tk < s_new)
        gate = _sigmoid(gl_ref[...])
        tail = nbf - 16
        for g in range(KV_HEADS):
            q = q_of(g)
            lane0 = 2 * kvw + g * HEAD_DIM
            k_new = pad_keys(new_ref[:, lane0:lane0 + HEAD_DIM])
            v_new = pad_keys(new_ref[:, lane0 + kvw:lane0 + kvw + HEAD_DIM])
            kb = lax.broadcasted_iota(jnp.int32, (16, LANES), 0)
            e = jnp.where(kb == nbp - tail, 1.0, 0.0).astype(BF)
            selx = lax.dot_general(sel_ref[g, tail:nbf, :].astype(BF), e, (((0,), (0,)), ((), ())),
                                   preferred_element_type=F32)
            s = lax.dot_general(q, k_new, (((1,), (1,)), ((), ())), preferred_element_type=F32)
            _online_update(s, (selx > 0.5) & new_ok, None, m_ref, l_ref, acc_ref, g, v_rows=v_new)
            o_sel = acc_ref[g] / l_ref[g][:, :HEAD_DIM]
            wl0 = g * HEAD_DIM
            kw_new = pad_keys(wnew_ref[:, wl0:wl0 + HEAD_DIM])
            vw_new = pad_keys(wnew_ref[:, kvw + wl0:kvw + wl0 + HEAD_DIM])
            sb = jnp.dot(q, wbuf_ref[0, g].astype(BF), preferred_element_type=F32)
            sn = lax.dot_general(q, kw_new, (((1,), (1,)), ((), ())), preferred_element_type=F32)
            jb = lax.broadcasted_iota(jnp.int32, (rq, wlen), 1)
            tq_b = lax.broadcasted_iota(jnp.int32, (rq, wlen), 0) & (s_new - 1)
            ok_b = (past - wlen + jb > past + tq_b - WINDOW) & (past - wlen + jb >= 0)
            sb = jnp.where(ok_b, sb, NEG)
            sn = jnp.where(new_ok, sn, NEG)
            mx = jnp.maximum(jnp.max(sb, axis=1, keepdims=True), jnp.max(sn, axis=1, keepdims=True))
            pb = jnp.where(ok_b, jnp.exp2(sb - mx), 0.0)
            pn = jnp.where(new_ok, jnp.exp2(sn - mx), 0.0)
            den = jnp.sum(pb, axis=1, keepdims=True) + jnp.sum(pn, axis=1, keepdims=True)
            o_win = (lax.dot_general(pb.astype(BF), wbuf_ref[1, g].astype(BF), (((1,), (1,)), ((), ())),
                                     preferred_element_type=F32)
                     + jnp.dot(pn.astype(BF), vw_new, preferred_element_type=F32)) / den
            o_cmp = ocmp_ref[g]
            for r in range(GQA):
                rows = slice(r * s_new, (r + 1) * s_new)
                c = g * LANES + 3 * r
                h = g * GQA + r
                o_ref[:, h * HEAD_DIM:(h + 1) * HEAD_DIM] = (
                    gate[:, c:c + 1] * o_cmp[rows] + gate[:, c + 1:c + 2] * o_sel[rows]
                    + gate[:, c + 2:c + 3] * o_win[rows]).astype(o_ref.dtype)


def _nsa_sample(cache_t, layer, page_table, q, kc, vc, nsa_new, win_new, wbuf_t, gl, n_pages, past, s_new):
    batch, ppb = page_table.shape
    kvw = KV_HEADS * HEAD_DIM
    qw = N_HEADS * HEAD_DIM
    nbp = past // NSA_BLOCK
    nbf = nbp + 8
    groups = LANES // (KV_HEADS * s_new)
    assert groups * KV_HEADS * s_new == LANES and nbf % groups == 0 and (nbf // groups) % 6 == 0
    wlen = wbuf_t.shape[-1]
    rq = GQA * s_new
    steps = ppb // n_pages

    def page_spec(p):
        return pl.BlockSpec((None, None, 2, KV_HEADS, HEAD_DIM, PAGE_SIZE),
                            lambda b, s, pt: (layer, pt[b, s * n_pages + p], 1, 0, 0, 0))

    per_b = lambda shape: pl.BlockSpec((None,) + shape, lambda b, s, pt: (b,) + (0,) * len(shape))
    rows_b = lambda w: pl.BlockSpec((s_new, w), lambda b, s, pt: (b, 0))
    return pl.pallas_call(
        functools.partial(_nsa_sample_kernel, n_pages=n_pages, past=past, s_new=s_new),
        grid_spec=pltpu.PrefetchScalarGridSpec(
            num_scalar_prefetch=1, grid=(batch, steps),
            in_specs=[
                per_b((N_HEADS, s_new, HEAD_DIM)),
                per_b((nbp, kvw)), per_b((nbp, kvw)),
                rows_b(4 * kvw), rows_b(2 * kvw),
                pl.BlockSpec((None, None, 2, KV_HEADS, HEAD_DIM, wlen), lambda b, s, pt: (layer, b, 0, 0, 0, 0)),
                rows_b(KV_HEADS * LANES),
            ] + [page_spec(p) for p in range(n_pages)],
            out_specs=rows_b(qw),
            scratch_shapes=[
                pltpu.VMEM((nbf, kvw), F32), pltpu.VMEM((nbf, kvw), F32),
                pltpu.VMEM((nbf, LANES), F32),
                pltpu.VMEM((KV_HEADS, nbf, rq), F32),
                pltpu.VMEM((KV_HEADS, rq, LANES), F32),
                pltpu.VMEM((KV_HEADS, rq, LANES), F32),
                pltpu.VMEM((KV_HEADS, rq, HEAD_DIM), F32),
                pltpu.VMEM((KV_HEADS, rq, HEAD_DIM), F32),
            ]),
        out_shape=jax.ShapeDtypeStruct((batch * s_new, qw), BF),
        compiler_params=_params("arbitrary", "arbitrary"),
        name="nsa_sample",
    )(page_table, q, kc, vc, nsa_new, win_new, wbuf_t, gl, *([cache_t] * n_pages))


def _rope_tables(pos):
    half = HEAD_DIM // 2
    freq = ROPE_THETA ** (-jnp.arange(half, dtype=F32) / half)
    ang = pos.astype(F32)[:, None] * freq[None, :]
    cos, sin = jnp.cos(ang), jnp.sin(ang)
    return jnp.tile(cos, (1, LANES // half)), jnp.tile(jnp.concatenate([-sin, sin], axis=1), (1, LANES // HEAD_DIM))


def _trunk(groups, mod4, mod5, wts):
    (g_mix, g_ffn, g_final, w_in_even_t, w_gate_t, sconv_w, w_out_even, w_in_odd, cconv_w, cconv_b, c_ln_g,
     c_ln_b, d_ln_g, d_ln_b, d_ws, d_bs_t, w_out_odd, w_up, ffn_conv_w, w_down) = wts
    prompt, sample = groups
    depth = g_mix.shape[0]
    d = prompt["x"].shape[1]
    sp = prompt["seq"]
    a_w = sconv_w.shape[-1]
    dff = ffn_conv_w.shape[-1]
    kv_cols = 3 * a_w + N_HEADS * HEAD_DIM + 6 * KV_HEADS * HEAD_DIM
    mm_tm = 1024
    down_tm = 512
    out_tn = 1024

    def prompt_gate(l, which, tm, tn):
        return pl.BlockSpec((None, None, None, 1, tn), lambda j, i: (l, which, (i * tm) // sp, 0, j))

    def sample_gate(l, which):
        b0 = sample["b_off"]
        return jnp.repeat(mod4[l, which, b0:b0 + sample["batch"]], sample["seq"], axis=0)

    def norm(xs, g, l, which):
        return [_norm_mod(x, g, mod5, l, which, grp["b_off"], grp["tm"], grp["tpb"]) for x, grp in zip(xs, groups)]

    xs = [prompt["x"], sample["x"]]
    new = [dict(nsa=[], win=[], s=[], c=[], dv=[], f=[]) for _ in groups]
    for l in range(depth):
        i = l // 2
        h = norm(xs, g_mix, l, 0)
        if l % 2 == 0:
            zs = _mm([h[0]], w_in_even_t, i, kv_cols, mm_tm, kv_cols // 4, wt=True, second=([h[1]], None, None),
                     name="in_even")
            gls = _mm([h[0]], w_gate_t, i, KV_HEADS * LANES, mm_tm, 512, wt=True, second=([h[1]], None, None),
                      name="gate_logits")
            mixed = []
            for z, gl, grp, out in zip(zs, gls, groups, new):
                mix_a, sb = _mixer_a(z, sconv_w, i, grp["sconv"][i], grp["tm"], grp["tpb"])
                o_b, nsa_rows, win_state = grp["nsa"](i, z, gl)
                mixed.append([mix_a, o_b])
                out["s"].append(sb)
                out["nsa"].append(nsa_rows)
                out["win"].append(win_state)
            w_out = w_out_even
        else:
            zs = _mm([h[0]], w_in_odd, i, w_in_odd.shape[-1], mm_tm, out_tn, second=([h[1]], None, None),
                     name="in_odd")
            mixed = []
            for z, grp, out in zip(zs, groups, new):
                mix, cb, v = _odd_post(z, i, grp["cconv"][i], cconv_w, cconv_b, c_ln_g, c_ln_b, d_ln_g, d_ln_b,
                                       d_ws, d_bs_t, grp["tm"], grp["tpb"])
                mixed.append([mix])
                out["c"].append(cb)
                out["dv"].append(v)
            w_out = w_out_odd
        xs = list(_mm(mixed[0], w_out, i, d, mm_tm, out_tn, res=xs[0], gate_spec=prompt_gate(l, 2, mm_tm, out_tn),
                      gate=mod5, second=(mixed[1], xs[1], sample_gate(l, 2)), name="out_proj"))
        h = norm(xs, g_ffn, l, 3)
        act_p, fb_p, a_s, g_s = _ffn_up(h[0], w_up, ffn_conv_w, l, prompt["ffn"][l], mm_tm, sp // mm_tm, 512, h[1])
        act_s, fb_s = _ffn_act(a_s, g_s, ffn_conv_w, l, sample["ffn"][l], sample["tm"], sample["tpb"], dff)
        new[0]["f"].append(fb_p)
        new[1]["f"].append(fb_s)
        xs = list(_mm([act_p], w_down, l, d, down_tm, 512, res=xs[0], gate_spec=prompt_gate(l, 5, down_tm, 512),
                      gate=mod5, second=([act_s], xs[1], sample_gate(l, 5)), name="ffn_down"))
    ys = [_final_norm(x, g_final, grp["tm"]) for x, grp in zip(xs, groups)]
    return ys, new


def kernel(x_prompt, x_sample, cache_nsa_kv, state_win_kv, state_sconv, state_cconv, state_ffn_conv, page_table, c_prompt, c_sample, g_mix, g_ffn, g_final, w_ada, b_ada, w_in_even, sconv_w, w_out_even, w_in_odd, cconv_w, cconv_b, c_ln_g, c_ln_b, d_ln_g, d_ln_b, d_ws, d_bs, w_out_odd, w_up, ffn_conv_w, w_down):
    bp, sp, d = x_prompt.shape
    bs, ss, _ = x_sample.shape
    depth = g_mix.shape[0]
    n_even = w_in_even.shape[0]
    past = page_table.shape[1] * PAGE_SIZE
    a_w = sconv_w.shape[-1]
    dff = ffn_conv_w.shape[-1]
    kvw = KV_HEADS * HEAD_DIM
    dt = x_prompt.dtype

    rows = -(-(bp + bs) // 8) * 8
    c_all = jnp.concatenate([c_prompt, c_sample, jnp.zeros((rows - bp - bs, d), dt)], axis=0)
    mod4 = _ada(c_all, w_ada, b_ada)
    mod5 = mod4.reshape(depth, 6, rows, 1, d)

    gate_c0 = 3 * a_w + N_HEADS * HEAD_DIM + 6 * kvw
    w_in_even_t = jnp.swapaxes(w_in_even, 1, 2)
    wg = w_in_even_t[:, gate_c0:, :].reshape(n_even, KV_HEADS, GQA * 3, d)
    w_gate_t = jnp.pad(wg, ((0, 0), (0, 0), (0, LANES - GQA * 3), (0, 0))).reshape(n_even, KV_HEADS * LANES, d)

    wts = (g_mix, g_ffn, g_final, w_in_even_t, w_gate_t, sconv_w, w_out_even, w_in_odd, cconv_w, cconv_b, c_ln_g,
           c_ln_b, d_ln_g, d_ln_b, d_ws, jnp.swapaxes(d_bs, 1, 2), w_out_odd, w_up, ffn_conv_w, w_down)

    tm_p = 512
    tq = 256
    cos_p, sin_p = _rope_tables(jnp.arange(sp, dtype=jnp.int32))

    def prompt_nsa(i, z, gl):
        qt, nsat, wint, kv, ksa, kc, vc = _rope_prompt(z, cos_p, sin_p, bp, sp, tm_p)
        o_b = _nsa_prompt(qt, kc, vc, ksa, kv, gl, bp, sp, tq)
        keep = min(WINDOW, sp)
        nsa_rows = jnp.transpose(nsat, (0, 4, 1, 2, 3))
        win_state = jnp.transpose(wint[..., sp - keep:], (0, 4, 1, 2, 3))
        return o_b, nsa_rows, win_state

    zeros = lambda n, r, w: jnp.zeros((n, bp, r, w), dt)
    prompt = dict(x=x_prompt.reshape(bp * sp, d), b_off=0, batch=bp, seq=sp, tm=tm_p, tpb=sp // tm_p,
                  sconv=zeros(n_even, 2, a_w), cconv=zeros(depth // 2, CCONV_W - 1, a_w), ffn=zeros(depth, 2, dff),
                  nsa=prompt_nsa)

    cache_t = jnp.transpose(cache_nsa_kv, (0, 1, 3, 4, 5, 2))
    wbuf_t = jnp.transpose(state_win_kv, (0, 1, 3, 4, 5, 2))
    cos_s, sin_s = _rope_tables(past + jnp.arange(ss, dtype=jnp.int32))

    def sample_nsa(i, z, gl):
        q, nsa_new, win_new = _rope_sample(z, cos_s, sin_s, bs, ss)
        kc, vc = _cmp_means(cache_t, i, page_table, 16)
        o_b = _nsa_sample(cache_t, i, page_table, q, kc, vc, nsa_new, win_new, wbuf_t, gl, 32, past, ss)
        win_new_t = jnp.transpose(win_new.reshape(bs, ss, 2, KV_HEADS, HEAD_DIM), (0, 2, 3, 4, 1))
        win_t = jnp.concatenate([wbuf_t[i], win_new_t], axis=-1)[..., ss:]
        win_state = jnp.transpose(win_t, (0, 4, 1, 2, 3))
        return o_b, nsa_new.reshape(bs, ss, 4, KV_HEADS, HEAD_DIM), win_state

    sample = dict(x=x_sample.reshape(bs * ss, d), b_off=bp, batch=bs, seq=ss, tm=ss, tpb=1,
                  sconv=state_sconv, cconv=state_cconv, ffn=state_ffn_conv, nsa=sample_nsa)

    (y_p, y_s), (new_p, new_s) = _trunk((prompt, sample), mod4, mod5, wts)

    nsa_p = [a.reshape(bp, sp, 4, KV_HEADS, HEAD_DIM) for a in new_p["nsa"]]
    win_p = [a.reshape(bp, -1, 2, KV_HEADS, HEAD_DIM) for a in new_p["win"]]
    dv_s = [a.reshape(bs, ss, -1) for a in new_s["dv"]]
    return (y_p.reshape(bp, sp, d), y_s.reshape(bs, ss, d), jnp.stack(nsa_p), jnp.stack(new_s["nsa"]),
            jnp.stack(win_p), jnp.stack(new_s["win"]), jnp.stack(new_p["s"]), jnp.stack(new_s["s"]),
            jnp.stack(new_p["c"]), jnp.stack(new_s["c"]), jnp.stack(dv_s), jnp.stack(new_p["f"]),
            jnp.stack(new_s["f"]))
```

```python
import functools
import math

import jax
import jax.numpy as jnp
from jax import lax
from jax.experimental import pallas as pl
from jax.experimental.pallas import tpu as pltpu

BF = jnp.bfloat16
F32 = jnp.float32

HEAD_DIM = 64
N_HEADS = 16
KV_HEADS = 4
GQA = N_HEADS // KV_HEADS
NSA_BLOCK = 64
N_SEL = 16
WINDOW = 512
PAGE_SIZE = 128
ROPE_THETA = 10000.0
CCONV_W = 31
D_CHUNK = 128
D_GROUPS = 4
EPS = 1e-6
NEG = -1e30

Q_SCALE = HEAD_DIM ** -0.5 * math.log2(math.e)

LANES = 128
VMEM_LIMIT = 56 * 1024 * 1024


def _params(*sem):
    return pltpu.CompilerParams(dimension_semantics=sem, vmem_limit_bytes=VMEM_LIMIT)


def _sigmoid(x):
    return 1.0 / (1.0 + jnp.exp(-x))


def _silu(x):
    return x * _sigmoid(x)


def _gelu_tanh(x):
    return 0.5 * x * (1.0 + jnp.tanh(math.sqrt(2.0 / math.pi) * (x + 0.044715 * (x * x * x))))


def _layernorm(x, g, b):
    mu = jnp.mean(x, axis=-1, keepdims=True)
    xc = x - mu
    return xc * lax.rsqrt(jnp.mean(xc * xc, axis=-1, keepdims=True) + EPS) * g + b


def _ada_kernel(c_ref, w_ref, b_ref, o_ref):
    ca = _silu(c_ref[...]).astype(BF)
    acc = jnp.dot(ca, w_ref[...].astype(BF), preferred_element_type=F32)
    o_ref[...] = acc + b_ref[...]


def _ada(c16, w_ada, b_ada):
    depth, d, n6 = w_ada.shape
    rows = c16.shape[0]
    tn = 1024
    per = d // tn
    return pl.pallas_call(
        _ada_kernel,
        grid=(depth, n6 // tn),
        in_specs=[
            pl.BlockSpec((rows, d), lambda l, j: (0, 0)),
            pl.BlockSpec((None, d, tn), lambda l, j: (l, 0, j)),
            pl.BlockSpec((None, 1, tn), lambda l, j: (l, 0, j)),
        ],
        out_specs=pl.BlockSpec((None, None, rows, tn), lambda l, j: (l, j // per, 0, j % per)),
        out_shape=jax.ShapeDtypeStruct((depth, 6, rows, d), F32),
        compiler_params=_params("arbitrary", "arbitrary"),
        name="ada",
    )(c16, w_ada, b_ada.reshape(depth, 1, n6))


def _norm_mod_kernel(x_ref, g_ref, sh_ref, sc_ref, o_ref):
    x = x_ref[...]
    y = x * lax.rsqrt(jnp.mean(x * x, axis=-1, keepdims=True) + EPS) * g_ref[...]
    o_ref[...] = (y * (1.0 + sc_ref[...]) + sh_ref[...]).astype(o_ref.dtype)


def _norm_mod(x, g, mod5, l, which, b_off, tm, tpb):
    m, d = x.shape
    mod_spec = lambda w: pl.BlockSpec((None, None, None, 1, d),
                                      lambda i: (l, w, b_off + i // tpb, 0, 0))
    return pl.pallas_call(
        _norm_mod_kernel,
        grid=(m // tm,),
        in_specs=[
            pl.BlockSpec((tm, d), lambda i: (i, 0)),
            pl.BlockSpec((None, 1, d), lambda i: (l, 0, 0)),
            mod_spec(which), mod_spec(which + 1),
        ],
        out_specs=pl.BlockSpec((tm, d), lambda i: (i, 0)),
        out_shape=jax.ShapeDtypeStruct((m, d), BF),
        compiler_params=_params("arbitrary"),
        name="norm_mod",
    )(x, g.reshape(g.shape[0], 1, d), mod5, mod5)


def _final_norm_kernel(x_ref, g_ref, o_ref):
    x = x_ref[...]
    o_ref[...] = x * lax.rsqrt(jnp.mean(x * x, axis=-1, keepdims=True) + EPS) * g_ref[...]


def _final_norm(x, g, tm):
    m, d = x.shape
    return pl.pallas_call(
        _final_norm_kernel,
        grid=(m // tm,),
        in_specs=[pl.BlockSpec((tm, d), lambda i: (i, 0)), pl.BlockSpec((1, d), lambda i: (0, 0))],
        out_specs=pl.BlockSpec((tm, d), lambda i: (i, 0)),
        out_shape=jax.ShapeDtypeStruct((m, d), F32),
        compiler_params=_params("arbitrary"),
        name="final_norm",
    )(x, g.reshape(1, d))


def _mm_kernel(*refs, k_sizes, res_gate, wt, second):
    n_a = len(k_sizes)
    per = n_a + (2 if res_gate else 0)
    w_ref = refs[0]
    groups = [refs[1:1 + per]] + ([refs[1 + per:1 + 2 * per]] if second else [])
    pos = 1 + per * len(groups)
    o_refs = refs[pos:pos + len(groups)]
    wb_ref = refs[pos + len(groups)]

    @pl.when(pl.program_id(1) == 0)
    def _():
        wb_ref[...] = w_ref[...].astype(BF)

    def product(group, o_ref):
        acc = None
        k0 = 0
        for a_ref, ks in zip(group[:n_a], k_sizes):
            if wt:
                part = lax.dot_general(a_ref[...], wb_ref[:, k0:k0 + ks], (((1,), (1,)), ((), ())),
                                       preferred_element_type=F32)
            else:
                part = jnp.dot(a_ref[...], wb_ref[k0:k0 + ks, :], preferred_element_type=F32)
            acc = part if acc is None else acc + part
            k0 += ks
        if res_gate:
            acc = group[n_a][...] + group[n_a + 1][...] * acc
        o_ref[...] = acc.astype(o_ref.dtype)

    product(groups[0], o_refs[0])
    if second:
        @pl.when(pl.program_id(1) == pl.num_programs(1) - 1)
        def _():
            product(groups[1], o_refs[1])


def _mm(a_list, w, l, n, tm, tn, wt=False, res=None, gate_spec=None, gate=None, second=None, name="mm"):
    m = a_list[0].shape[0]
    k_sizes = tuple(a.shape[1] for a in a_list)
    k = sum(k_sizes)
    assert w.shape[2 if wt else 1] == k and n % tn == 0 and m % tm == 0
    if wt:
        in_specs = [pl.BlockSpec((None, tn, k), lambda j, i: (l, j, 0))]
    else:
        in_specs = [pl.BlockSpec((None, k, tn), lambda j, i: (l, 0, j))]
    args = [w]
    in_specs += [pl.BlockSpec((tm, ks), lambda j, i: (i, 0)) for ks in k_sizes]
    args += list(a_list)
    if res is not None:
        in_specs += [pl.BlockSpec((tm, tn), lambda j, i: (i, j)), gate_spec]
        args += [res, gate]
    out_specs = [pl.BlockSpec((tm, tn), lambda j, i: (i, j))]
    out_shape = [jax.ShapeDtypeStruct((m, n), F32)]
    if second is not None:
        a_list2, res2, gate2 = second
        m2 = a_list2[0].shape[0]
        in_specs += [pl.BlockSpec((m2, ks), lambda j, i: (0, 0)) for ks in k_sizes]
        args += list(a_list2)
        if res is not None:
            in_specs += [pl.BlockSpec((m2, tn), lambda j, i: (0, j))] * 2
            args += [res2, gate2]
        out_specs.append(pl.BlockSpec((m2, tn), lambda j, i: (0, j)))
        out_shape.append(jax.ShapeDtypeStruct((m2, n), F32))
    outs = pl.pallas_call(
        functools.partial(_mm_kernel, k_sizes=k_sizes, res_gate=res is not None, wt=wt, second=second is not None),
        grid=(n // tn, m // tm),
        in_specs=in_specs,
        out_specs=out_specs,
        out_shape=out_shape,
        scratch_shapes=[pltpu.VMEM((tn, k) if wt else (k, tn), BF)],
        compiler_params=_params("arbitrary", "arbitrary"),
        name=name,
    )(*args)
    return outs if second is not None else outs[0]


def _conv3(u, p, w):
    row = lax.broadcasted_iota(jnp.int32, u.shape, 0)
    um1 = jnp.where(row == 0, p[1:2], pltpu.roll(u, 1, 0))
    um2 = jnp.where(row == 0, p[0:1], jnp.where(row == 1, p[1:2], pltpu.roll(u, 2, 0)))
    return w[0:1] * um2 + w[1:2] * um1 + w[2:3] * u


def _prev_rows(i, tpb, prev_ref, carry_ref):
    if tpb == 1:
        return prev_ref[...]
    return jnp.where(i % tpb == 0, prev_ref[...], carry_ref[6:8, :])


def _mixer_a_kernel(ain_ref, ab_ref, ac_ref, w_ref, prev_ref, o_ref, st_ref, carry_ref, *, tm, tpb):
    i = pl.program_id(0)
    u = ac_ref[...] * ain_ref[...]
    p = _prev_rows(i, tpb, prev_ref, carry_ref)
    o_ref[...] = (ab_ref[...] * _conv3(u, p, w_ref[...])).astype(o_ref.dtype)
    st_ref[...] = u[tm - 2:tm]
    if tpb > 1:
        carry_ref[...] = u[tm - 8:tm]


def _mixer_a(z, sconv_w, l, prev, tm, tpb):
    m = z.shape[0]
    c = prev.shape[-1]
    nb = prev.shape[0]
    return pl.pallas_call(
        functools.partial(_mixer_a_kernel, tm=tm, tpb=tpb),
        grid=(m // tm,),
        in_specs=[
            pl.BlockSpec((tm, c), lambda i: (i, 0)),
            pl.BlockSpec((tm, c), lambda i: (i, 1)),
            pl.BlockSpec((tm, c), lambda i: (i, 2)),
            pl.BlockSpec((None, 3, c), lambda i: (l, 0, 0)),
            pl.BlockSpec((None, 2, c), lambda i: (i // tpb, 0, 0)),
        ],
        out_specs=[
            pl.BlockSpec((tm, c), lambda i: (i, 0)),
            pl.BlockSpec((None, 2, c), lambda i: (i // tpb, 0, 0)),
        ],
        out_shape=[jax.ShapeDtypeStruct((m, c), BF), jax.ShapeDtypeStruct((nb, 2, c), F32)],
        scratch_shapes=[pltpu.VMEM((8, c), F32)],
        compiler_params=_params("arbitrary"),
        name="mixer_a",
    )(z, z, z, sconv_w, prev)


def _ffn_act_kernel(a_ref, g_ref, w_ref, prev_ref, o_ref, st_ref, carry_ref, *, tm, tpb):
    i = pl.program_id(1)
    a = a_ref[...]
    p = _prev_rows(i, tpb, prev_ref, carry_ref)
    o_ref[...] = (_silu(_conv3(a, p, w_ref[...])) * g_ref[...]).astype(o_ref.dtype)
    st_ref[...] = a[tm - 2:tm]
    if tpb > 1:
        carry_ref[...] = a[tm - 8:tm]


def _ffn_act(za, zg, conv_w, l, prev, tm, tpb, tn):
    m = za.shape[0]
    dff = prev.shape[-1]
    nb = prev.shape[0]
    nj = dff // tn
    return pl.pallas_call(
        functools.partial(_ffn_act_kernel, tm=tm, tpb=tpb),
        grid=(nj, m // tm),
        in_specs=[
            pl.BlockSpec((tm, tn), lambda j, i: (i, j)),
            pl.BlockSpec((tm, tn), lambda j, i: (i, j)),
            pl.BlockSpec((None, 3, tn), lambda j, i: (l, 0, j)),
            pl.BlockSpec((None, 2, tn), lambda j, i: (i // tpb, 0, j)),
        ],
        out_specs=[
            pl.BlockSpec((tm, tn), lambda j, i: (i, j)),
            pl.BlockSpec((None, 2, tn), lambda j, i: (i // tpb, 0, j)),
        ],
        out_shape=[jax.ShapeDtypeStruct((m, dff), BF), jax.ShapeDtypeStruct((nb, 2, dff), F32)],
        scratch_shapes=[pltpu.VMEM((8, tn), F32)],
        compiler_params=_params("arbitrary", "arbitrary"),
        name="ffn_act",
    )(za, zg, conv_w, prev)


def _ffn_up_kernel(h_ref, wa_ref, wg_ref, cw_ref, prev_ref, h2_ref, o_ref, st_ref, a2_ref, g2_ref,
                   wab_ref, wgb_ref, carry_ref, *, tm, tpb):
    i = pl.program_id(1)

    @pl.when(i == 0)
    def _():
        wab_ref[...] = wa_ref[...].astype(BF)
        wgb_ref[...] = wg_ref[...].astype(BF)

    h = h_ref[...]
    a = jnp.dot(h, wab_ref[...], preferred_element_type=F32)
    g = jnp.dot(h, wgb_ref[...], preferred_element_type=F32)
    p = _prev_rows(i, tpb, prev_ref, carry_ref)
    o_ref[...] = (_silu(_conv3(a, p, cw_ref[...])) * g).astype(o_ref.dtype)
    st_ref[...] = a[tm - 2:tm]
    if tpb > 1:
        carry_ref[...] = a[tm - 8:tm]

    @pl.when(i == pl.num_programs(1) - 1)
    def _():
        a2_ref[...] = jnp.dot(h2_ref[...], wab_ref[...], preferred_element_type=F32)
        g2_ref[...] = jnp.dot(h2_ref[...], wgb_ref[...], preferred_element_type=F32)


def _ffn_up(h, w_up, conv_w, l, prev, tm, tpb, tn, h2):
    m, d = h.shape
    m2 = h2.shape[0]
    dff = prev.shape[-1]
    nb = prev.shape[0]
    nj = dff // tn
    small = pl.BlockSpec((m2, tn), lambda j, i: (0, j))
    return pl.pallas_call(
        functools.partial(_ffn_up_kernel, tm=tm, tpb=tpb),
        grid=(nj, m // tm),
        in_specs=[
            pl.BlockSpec((tm, d), lambda j, i: (i, 0)),
            pl.BlockSpec((None, d, tn), lambda j, i: (l, 0, j)),
            pl.BlockSpec((None, d, tn), lambda j, i: (l, 0, j + nj)),
            pl.BlockSpec((None, 3, tn), lambda j, i: (l, 0, j)),
            pl.BlockSpec((None, 2, tn), lambda j, i: (i // tpb, 0, j)),
            pl.BlockSpec((m2, d), lambda j, i: (0, 0)),
        ],
        out_specs=[
            pl.BlockSpec((tm, tn), lambda j, i: (i, j)),
            pl.BlockSpec((None, 2, tn), lambda j, i: (i // tpb, 0, j)),
            small, small,
        ],
        out_shape=[jax.ShapeDtypeStruct((m, dff), BF), jax.ShapeDtypeStruct((nb, 2, dff), F32),
                   jax.ShapeDtypeStruct((m2, dff), F32), jax.ShapeDtypeStruct((m2, dff), F32)],
        scratch_shapes=[pltpu.VMEM((d, tn), BF), pltpu.VMEM((d, tn), BF), pltpu.VMEM((8, tn), F32)],
        compiler_params=_params("arbitrary", "arbitrary"),
        name="ffn_up_fused",
    )(h, w_up, w_up, conv_w, prev, h2)


def _odd_post_kernel(ca_ref, cg_ref, du_ref, dv_ref, cw_ref, cb_ref, clg_ref, clb_ref, dlg_ref,
                     dlb_ref, ws_ref, bst_ref, prev_ref, o_ref, st_ref, v_ref, cbuf, wbuf, *, tm, tpb):
    i = pl.program_id(0)
    cw = cw_ref.shape[-1]
    c = ca_ref[...] * _sigmoid(cg_ref[...])
    if tpb == 1:
        cbuf[2:32, :] = prev_ref[...]
    else:
        @pl.when(i % tpb == 0)
        def _():
            cbuf[2:32, :] = prev_ref[...]

        @pl.when(i % tpb != 0)
        def _():
            cbuf[0:32, :] = cbuf[tm:tm + 32, :]
    cbuf[32:32 + tm, :] = c
    acc = None
    for r in range(8):
        taps = range(r, CCONV_W, 8)
        rows = tm + 8 * (len(taps) - 1)
        wbuf[0:rows, :] = cbuf[2 + r:2 + r + rows, :]
        for j, k in enumerate(taps):
            term = cw_ref[k:k + 1, :] * wbuf[8 * j:8 * j + tm, :]
            acc = term if acc is None else acc + term
    st_ref[...] = cbuf[tm + 2:tm + 32, :]
    o_ref[:, 0:cw] = _silu(_layernorm(acc + cb_ref[...], clg_ref[...], clb_ref[...])).astype(o_ref.dtype)

    u = _gelu_tanh(du_ref[...])
    v = _layernorm(_gelu_tanh(dv_ref[...]), dlg_ref[...], dlb_ref[...])
    v_ref[...] = v
    gw = cw // D_GROUPS
    trow = lax.broadcasted_iota(jnp.int32, (D_CHUNK, D_CHUNK), 0)
    tcol = lax.broadcasted_iota(jnp.int32, (D_CHUNK, D_CHUNK), 1)
    rows = min(tm, D_CHUNK)
    for ch in range(max(1, tm // D_CHUNK)):
        r0 = ch * D_CHUNK
        vch = v[r0:r0 + rows]
        if rows < D_CHUNK:
            vch = jnp.concatenate([vch, jnp.zeros((D_CHUNK - rows, cw), F32)], axis=0)
        vch = vch.astype(BF)
        for g in range(D_GROUPS):
            wg = jnp.where(tcol <= trow, ws_ref[g], 0.0).astype(BF)
            zz = jnp.dot(wg, vch[:, g * gw:(g + 1) * gw], preferred_element_type=F32)
            zz = zz + bst_ref[:, g:g + 1]
            o_ref[r0:r0 + rows, cw + g * gw:cw + (g + 1) * gw] = (
                u[r0:r0 + rows, g * gw:(g + 1) * gw] * zz[0:rows]).astype(o_ref.dtype)


def _odd_post(z, i_odd, prev, cconv_w, cconv_b, c_ln_g, c_ln_b, d_ln_g, d_ln_b, d_ws, d_bs_t, tm, tpb, keep_v):
    m = z.shape[0]
    c = prev.shape[-1]
    nb = prev.shape[0]
    vec = lambda: pl.BlockSpec((None, 1, c), lambda i: (i_odd, 0, 0))
    r3 = lambda a: a.reshape(a.shape[0], 1, c)
    return pl.pallas_call(
        functools.partial(_odd_post_kernel, tm=tm, tpb=tpb),
        grid=(m // tm,),
        in_specs=[
            pl.BlockSpec((tm, c), lambda i: (i, 0)),
            pl.BlockSpec((tm, c), lambda i: (i, 1)),
            pl.BlockSpec((tm, c), lambda i: (i, 2)),
            pl.BlockSpec((tm, c), lambda i: (i, 3)),
            pl.BlockSpec((None, CCONV_W, c), lambda i: (i_odd, 0, 0)),
            vec(), vec(), vec(), vec(), vec(),
            pl.BlockSpec((None, D_GROUPS, D_CHUNK, D_CHUNK), lambda i: (i_odd, 0, 0, 0)),
            pl.BlockSpec((None, D_CHUNK, D_GROUPS), lambda i: (i_odd, 0, 0)),
            pl.BlockSpec((None, CCONV_W - 1, c), lambda i: (i // tpb, 0, 0)),
        ],
        out_specs=[
            pl.BlockSpec((tm, 2 * c), lambda i: (i, 0)),
            pl.BlockSpec((None, CCONV_W - 1, c), lambda i: (i // tpb, 0, 0)),
            pl.BlockSpec((tm, c), (lambda i: (i, 0)) if keep_v else (lambda i: (0, 0))),
        ],
        out_shape=[jax.ShapeDtypeStruct((m, 2 * c), BF),
                   jax.ShapeDtypeStruct((nb, CCONV_W - 1, c), F32),
                   jax.ShapeDtypeStruct((m if keep_v else tm, c), F32)],
        scratch_shapes=[pltpu.VMEM((32 + tm, c), F32), pltpu.VMEM((24 + tm, c), F32)],
        compiler_params=_params("arbitrary"),
        name="odd_post",
    )(z, z, z, z, cconv_w, r3(cconv_b), r3(c_ln_g), r3(c_ln_b), r3(d_ln_g), r3(d_ln_b), d_ws, d_bs_t, prev)


def _rope128(x, cos, sin_signed):
    lane = lax.broadcasted_iota(jnp.int32, x.shape, 1)
    swapped = jnp.where((lane & (HEAD_DIM - 1)) < HEAD_DIM // 2,
                        pltpu.roll(x, LANES - HEAD_DIM // 2, 1), pltpu.roll(x, HEAD_DIM // 2, 1))
    return x * cos + swapped * sin_signed


def _rope_slab(ref, c0, width, cos, sin_signed):
    return jnp.concatenate(
        [_rope128(ref[:, c0 + k * LANES:c0 + (k + 1) * LANES], cos, sin_signed) for k in range(width // LANES)],
        axis=1)


def _rope_prompt_kernel(zq_ref, zn_ref, zw_ref, cos_ref, sin_ref, qt_ref, nsat_ref, wint_ref, kv_ref, ksa_ref,
                        kc_ref, vc_ref, *, ts, tpb):
    cos, sin = cos_ref[...], sin_ref[...]
    kvw = KV_HEADS * HEAD_DIM
    scale = Q_SCALE
    for k in range(N_HEADS * HEAD_DIM // LANES):
        rt = (_rope128(zq_ref[:, k * LANES:(k + 1) * LANES], cos, sin) * scale).T
        qt_ref[2 * k] = rt[:HEAD_DIM].astype(qt_ref.dtype)
        qt_ref[2 * k + 1] = rt[HEAD_DIM:].astype(qt_ref.dtype)
    k_cmp = _rope_slab(zn_ref, 0, kvw, cos, sin)
    v_cmp = zn_ref[:, kvw:2 * kvw]
    k_sel = _rope_slab(zn_ref, 2 * kvw, kvw, cos, sin)
    v_sel = zn_ref[:, 3 * kvw:4 * kvw]
    k_win = _rope_slab(zw_ref, 0, kvw, cos, sin)
    v_win = zw_ref[:, kvw:2 * kvw]
    for t, slab in enumerate((k_cmp, v_cmp, k_sel, v_sel)):
        nsat_ref[t] = slab.T.reshape(KV_HEADS, HEAD_DIM, ts)
    for t, slab in enumerate((k_win, v_win)):
        wint_ref[t] = slab.T.reshape(KV_HEADS, HEAD_DIM, ts)
    for t, slab in enumerate((v_sel, k_win, v_win)):
        for g in range(KV_HEADS):
            kv_ref[t, g] = slab[:, g * HEAD_DIM:(g + 1) * HEAD_DIM].astype(kv_ref.dtype)
    row0 = (pl.program_id(0) % tpb) * ts
    blk_of_row = (row0 + lax.broadcasted_iota(jnp.int32, (ts, HEAD_DIM), 0)) // NSA_BLOCK
    onehot = jnp.where(blk_of_row == lax.broadcasted_iota(jnp.int32, (ts, HEAD_DIM), 1), 1.0, 0.0)
    for g in range(KV_HEADS):
        ksa_ref[g] = jnp.concatenate([k_sel[:, g * HEAD_DIM:(g + 1) * HEAD_DIM], onehot], axis=1).astype(ksa_ref.dtype)
    nblk = ts // NSA_BLOCK
    kc = jnp.sum(k_cmp.reshape(nblk, NSA_BLOCK, kvw), axis=1) * (1.0 / NSA_BLOCK)
    vc = jnp.sum(v_cmp.reshape(nblk, NSA_BLOCK, kvw), axis=1) * (1.0 / NSA_BLOCK)
    for g in range(KV_HEADS):
        kc_ref[g] = kc[:, g * HEAD_DIM:(g + 1) * HEAD_DIM]
        vc_ref[g] = vc[:, g * HEAD_DIM:(g + 1) * HEAD_DIM]


def _rope_prompt(z, cos, sin, batch, seq, ts):
    tpb = seq // ts
    qw = N_HEADS * HEAD_DIM
    nblk = ts // NSA_BLOCK
    assert seq // NSA_BLOCK <= HEAD_DIM
    return pl.pallas_call(
        functools.partial(_rope_prompt_kernel, ts=ts, tpb=tpb),
        grid=(batch * tpb,),
        in_specs=[
            pl.BlockSpec((ts, qw), lambda i: (i, 3)),
            pl.BlockSpec((ts, qw), lambda i: (i, 4)),
            pl.BlockSpec((ts, qw // 2), lambda i: (i, 10)),
            pl.BlockSpec((ts, LANES), lambda i: (i % tpb, 0)),
            pl.BlockSpec((ts, LANES), lambda i: (i % tpb, 0)),
        ],
        out_specs=[
            pl.BlockSpec((None, N_HEADS, HEAD_DIM, ts), lambda i: (i // tpb, 0, 0, i % tpb)),
            pl.BlockSpec((None, 4, KV_HEADS, HEAD_DIM, ts), lambda i: (i // tpb, 0, 0, 0, i % tpb)),
            pl.BlockSpec((None, 2, KV_HEADS, HEAD_DIM, ts), lambda i: (i // tpb, 0, 0, 0, i % tpb)),
            pl.BlockSpec((None, 3, KV_HEADS, ts, HEAD_DIM), lambda i: (i // tpb, 0, 0, i % tpb, 0)),
            pl.BlockSpec((None, KV_HEADS, ts, 2 * HEAD_DIM), lambda i: (i // tpb, 0, i % tpb, 0)),
            pl.BlockSpec((None, KV_HEADS, nblk, HEAD_DIM), lambda i: (i // tpb, 0, i % tpb, 0)),
            pl.BlockSpec((None, KV_HEADS, nblk, HEAD_DIM), lambda i: (i // tpb, 0, i % tpb, 0)),
        ],
        out_shape=[
            jax.ShapeDtypeStruct((batch, N_HEADS, HEAD_DIM, seq), BF),
            jax.ShapeDtypeStruct((batch, 4, KV_HEADS, HEAD_DIM, seq), F32),
            jax.ShapeDtypeStruct((batch, 2, KV_HEADS, HEAD_DIM, seq), F32),
            jax.ShapeDtypeStruct((batch, 3, KV_HEADS, seq, HEAD_DIM), BF),
            jax.ShapeDtypeStruct((batch, KV_HEADS, seq, 2 * HEAD_DIM), BF),
            jax.ShapeDtypeStruct((batch, KV_HEADS, seq // NSA_BLOCK, HEAD_DIM), F32),
            jax.ShapeDtypeStruct((batch, KV_HEADS, seq // NSA_BLOCK, HEAD_DIM), F32),
        ],
        compiler_params=_params("arbitrary"),
        name="rope_prompt",
    )(z, z, z, cos, sin)


def _rope_sample_kernel(zq_ref, zn_ref, zw_ref, cos_ref, sin_ref, q_ref, nsa_ref, win_ref):
    cos, sin = cos_ref[...], sin_ref[...]
    kvw = KV_HEADS * HEAD_DIM
    scale = Q_SCALE
    for k in range(N_HEADS * HEAD_DIM // LANES):
        r = _rope128(zq_ref[:, k * LANES:(k + 1) * LANES], cos, sin) * scale
        q_ref[2 * k] = r[:, :HEAD_DIM]
        q_ref[2 * k + 1] = r[:, HEAD_DIM:]
    nsa_ref[:, 0:kvw] = _rope_slab(zn_ref, 0, kvw, cos, sin)
    nsa_ref[:, kvw:2 * kvw] = zn_ref[:, kvw:2 * kvw]
    nsa_ref[:, 2 * kvw:3 * kvw] = _rope_slab(zn_ref, 2 * kvw, kvw, cos, sin)
    nsa_ref[:, 3 * kvw:4 * kvw] = zn_ref[:, 3 * kvw:4 * kvw]
    win_ref[:, 0:kvw] = _rope_slab(zw_ref, 0, kvw, cos, sin)
    win_ref[:, kvw:2 * kvw] = zw_ref[:, kvw:2 * kvw]


def _rope_sample(z, cos, sin, batch, seq):
    qw = N_HEADS * HEAD_DIM
    return pl.pallas_call(
        _rope_sample_kernel,
        grid=(batch,),
        in_specs=[
            pl.BlockSpec((seq, qw), lambda i: (i, 3)),
            pl.BlockSpec((seq, qw), lambda i: (i, 4)),
            pl.BlockSpec((seq, qw // 2), lambda i: (i, 10)),
            pl.BlockSpec((seq, LANES), lambda i: (0, 0)),
            pl.BlockSpec((seq, LANES), lambda i: (0, 0)),
        ],
        out_specs=[
            pl.BlockSpec((None, N_HEADS, seq, HEAD_DIM), lambda i: (i, 0, 0, 0)),
            pl.BlockSpec((seq, qw), lambda i: (i, 0)),
            pl.BlockSpec((seq, qw // 2), lambda i: (i, 0)),
        ],
        out_shape=[
            jax.ShapeDtypeStruct((batch, N_HEADS, seq, HEAD_DIM), F32),
            jax.ShapeDtypeStruct((batch * seq, qw), F32),
            jax.ShapeDtypeStruct((batch * seq, qw // 2), F32),
        ],
        compiler_params=_params("arbitrary"),
        name="rope_sample",
    )(z, z, z, cos, sin)


def _cmp_branch(kc, vc, q, qpos, nq):
    nb = kc.shape[0]
    st = lax.dot_general(kc.astype(BF), q, (((1,), (1,)), ((), ())), preferred_element_type=F32)
    blk = lax.broadcasted_iota(jnp.int32, st.shape, 0)
    ok = (blk + 1) * NSA_BLOCK <= qpos + 1
    sm = jnp.where(ok, st, NEG)
    mx = jnp.max(sm, axis=0, keepdims=True)
    e = jnp.where(ok, jnp.exp2(sm - mx), 0.0)
    den = jnp.sum(e, axis=0, keepdims=True)
    pt = e / jnp.where(den > 0.0, den, 1.0)
    o_cmp = lax.dot_general(pt.astype(BF), vc.astype(BF), (((0,), (0,)), ((), ())), preferred_element_type=F32)
    imp = pt[:, 0:nq]
    for r in range(1, GQA):
        imp = imp + pt[:, r * nq:(r + 1) * nq]
    return o_cmp, imp


def _importance(imp, qpos_q):
    blk = lax.broadcasted_iota(jnp.int32, imp.shape, 0)
    cur = qpos_q // NSA_BLOCK
    forced = (blk == 0) | (blk == cur) | (blk == cur - 1)
    imp = jnp.where(forced, GQA + 1.0, imp)
    return jnp.where(blk <= cur, imp, -1.0)


def _select_topk(imp_ref, nb):
    imp = imp_ref[...]
    blk = lax.broadcasted_iota(jnp.int32, imp.shape, 0)

    def body(i, rank):
        row = imp_ref[pl.ds(i, 1), :]
        ahead = (row > imp) | ((row == imp) & (i < blk))
        return rank + jnp.where(ahead, 1.0, 0.0)

    rank = lax.fori_loop(0, nb, body, jnp.zeros(imp.shape, F32), unroll=8)
    return jnp.where(rank < float(N_SEL), 1.0, 0.0)


def _select_topk_packed(imp_ref, nb, width):
    groups = LANES // width
    per = nb // groups
    imp = imp_ref[...]
    blk = lax.broadcasted_iota(jnp.int32, imp.shape, 0)
    grp = lax.broadcasted_iota(jnp.int32, (1, LANES), 1) // width
    first = grp * per

    def body(i, rank):
        row = imp_ref[pl.ds(i, 1), :]
        for g in range(1, groups):
            row = jnp.where(grp == g, imp_ref[pl.ds(i + g * per, 1), :], row)
        ahead = (row > imp) | ((row == imp) & (i + first < blk))
        return rank + jnp.where(ahead, 1.0, 0.0)

    rank = lax.fori_loop(0, per, body, jnp.zeros(imp.shape, F32), unroll=6)
    total = rank[:, 0:width]
    for g in range(1, groups):
        total = total + rank[:, g * width:(g + 1) * width]
    return jnp.where(total < float(N_SEL), 1.0, 0.0)


def _expand_blocks(sel_t, n_keys, first_block):
    nbl = sel_t.shape[0]
    kb = lax.broadcasted_iota(jnp.int32, (nbl, n_keys), 1) // NSA_BLOCK + first_block
    nn = lax.broadcasted_iota(jnp.int32, (nbl, n_keys), 0)
    e = jnp.where(kb == nn, 1.0, 0.0).astype(BF)
    return lax.dot_general(sel_t.astype(BF), e, (((0,), (0,)), ((), ())), preferred_element_type=F32)


def _online_update(s, valid, vt, m_ref, l_ref, acc_ref, idx, v_rows=None):
    nk = s.shape[1]
    s = jnp.where(valid, s, NEG)
    m_prev = m_ref[idx]
    m_new = jnp.maximum(m_prev, jnp.max(s, axis=1, keepdims=True))
    alpha = jnp.exp2(m_prev - m_new)
    p = jnp.where(valid, jnp.exp2(s - jnp.concatenate([m_new] * (nk // LANES), axis=1)), 0.0)
    l_ref[idx] = alpha * l_ref[idx] + jnp.sum(p, axis=1, keepdims=True)
    if v_rows is None:
        pv = lax.dot_general(p.astype(BF), vt, (((1,), (1,)), ((), ())), preferred_element_type=F32)
    else:
        pv = jnp.dot(p.astype(BF), v_rows, preferred_element_type=F32)
    acc_ref[idx] = alpha[:, :HEAD_DIM] * acc_ref[idx] + pv
    m_ref[idx] = m_new


def _online_update_t(k_rows, qt, bias, v_rows, m_ref, l_ref, acc_ref, idx):
    st = jnp.dot(k_rows, qt, preferred_element_type=F32)
    nq = qt.shape[1] // GQA
    ps, alphas = [], []
    for r in range(GQA):
        cols = slice(r * nq, (r + 1) * nq)
        s_r = st[:, cols] if bias is None else st[:, cols] + bias
        m_prev = m_ref[idx, :, cols]
        m_new = jnp.maximum(m_prev, jnp.max(s_r, axis=0, keepdims=True))
        alpha = jnp.exp2(m_prev - m_new)
        p_r = jnp.exp2(s_r - m_new)
        l_ref[idx, :, cols] = alpha * l_ref[idx, :, cols] + jnp.sum(p_r, axis=0, keepdims=True)
        m_ref[idx, :, cols] = m_new
        ps.append(p_r.astype(BF))
        alphas.append(alpha)
    pv = lax.dot_general(v_rows, jnp.concatenate(ps, axis=1), (((0,), (0,)), ((), ())),
                         preferred_element_type=F32)
    acc_ref[idx] = jnp.concatenate(alphas, axis=1) * acc_ref[idx] + pv


def _nsa_prompt_kernel(qt_ref, kc_ref, vc_ref, ks_ref, vs_ref, kw_ref, vw_ref, gl_ref, o_ref,
                       imp_ref, m_ref, l_ref, acc_ref, *, tq, seq):
    qi = pl.program_id(2)
    q0 = qi * tq
    rq = GQA * tq
    nb = seq // NSA_BLOCK
    bpt = tq // NSA_BLOCK
    qt = jnp.concatenate([qt_ref[r] for r in range(GQA)], axis=1)

    st = jnp.dot(kc_ref[...].astype(BF), qt, preferred_element_type=F32)
    blk = lax.broadcasted_iota(jnp.int32, (nb, rq), 0)
    qpos_r = q0 + (lax.broadcasted_iota(jnp.int32, (nb, rq), 1) & (tq - 1))
    ok = (blk + 1) * NSA_BLOCK <= qpos_r + 1
    sm = jnp.where(ok, st, NEG)
    e = jnp.where(ok, jnp.exp2(sm - jnp.max(sm, axis=0, keepdims=True)), 0.0)
    den = jnp.sum(e, axis=0, keepdims=True)
    pt = e / jnp.where(den > 0.0, den, 1.0)
    o_cmp = lax.dot_general(vc_ref[...].astype(BF), pt.astype(BF), (((0,), (0,)), ((), ())),
                            preferred_element_type=F32)
    imp = pt[:, 0:tq]
    for r in range(1, GQA):
        imp = imp + pt[:, r * tq:(r + 1) * tq]
    imp_ref[...] = _importance(imp, q0 + lax.broadcasted_iota(jnp.int32, (nb, tq), 1))
    sel_bias = ((_select_topk(imp_ref, nb) - 1.0) * (-NEG)).astype(BF)
    q_sel = jnp.concatenate([qt, jnp.concatenate([sel_bias] * GQA, axis=1), jnp.zeros((HEAD_DIM - nb, rq), BF)],
                            axis=0)

    m_ref[...] = jnp.full(m_ref.shape, NEG, F32)
    l_ref[...] = jnp.zeros(l_ref.shape, F32)
    acc_ref[...] = jnp.zeros(acc_ref.shape, F32)

    krow = lax.broadcasted_iota(jnp.int32, (tq, tq), 0)
    qcol = lax.broadcasted_iota(jnp.int32, (tq, tq), 1)

    def rows_of(c):
        return pl.ds(pl.multiple_of(c * tq, tq), tq)

    def sel_chunk(c):
        _online_update_t(ks_ref[rows_of(c), :], q_sel, None, vs_ref[rows_of(c), :], m_ref, l_ref, acc_ref, 0)

    def sel_pair(j, carry):
        sel_chunk(2 * j)
        sel_chunk(2 * j + 1)
        return carry

    n_full = (WINDOW - tq) // tq
    n_far = jnp.maximum(qi - (n_full + 1), 0)
    lax.fori_loop(0, n_far // 2, sel_pair, 0)

    @pl.when(n_far % 2 == 1)
    def _():
        sel_chunk(n_far - 1)
    for rel in range(n_full + 1, 0, -1):
        @pl.when(qi >= rel)
        def _(rel=rel):
            c = qi - rel
            sel_chunk(c)
            bias = None if rel <= n_full else jnp.where(krow > qcol + (rel * tq - WINDOW), 0.0, NEG)
            _online_update_t(kw_ref[rows_of(c), :], qt, bias, vw_ref[rows_of(c), :], m_ref, l_ref, acc_ref, 1)
    causal_bias = jnp.where(krow <= qcol, 0.0, NEG)
    _online_update_t(ks_ref[rows_of(qi), :], q_sel, causal_bias, vs_ref[rows_of(qi), :], m_ref, l_ref, acc_ref, 0)
    _online_update_t(kw_ref[rows_of(qi), :], qt, causal_bias, vw_ref[rows_of(qi), :], m_ref, l_ref, acc_ref, 1)

    o_sel = acc_ref[0] * (1.0 / l_ref[0])
    o_win = acc_ref[1] * (1.0 / l_ref[1])
    gate_t = _sigmoid(gl_ref[...]).T
    outs = []
    for r in range(GQA):
        cols = slice(r * tq, (r + 1) * tq)
        o_t = (gate_t[3 * r:3 * r + 1] * o_cmp[:, cols] + gate_t[3 * r + 1:3 * r + 2] * o_sel[:, cols]
               + gate_t[3 * r + 2:3 * r + 3] * o_win[:, cols])
        outs.append(o_t.T)
    o_ref[...] = jnp.concatenate(outs, axis=1).astype(o_ref.dtype)


def _nsa_prompt(qt, kc, vc, ksa, kv, gl, batch, seq, tq):
    nq = seq // tq
    nb = seq // NSA_BLOCK
    rq = GQA * tq
    assert WINDOW % tq == 0
    kv_spec = lambda t: pl.BlockSpec((None, None, None, seq, HEAD_DIM), lambda b, g, i: (b, t, g, 0, 0))
    cmp_spec = pl.BlockSpec((None, None, nb, HEAD_DIM), lambda b, g, i: (b, g, 0, 0))
    return pl.pallas_call(
        functools.partial(_nsa_prompt_kernel, tq=tq, seq=seq),
        grid=(batch, KV_HEADS, nq),
        in_specs=[
            pl.BlockSpec((None, GQA, HEAD_DIM, tq), lambda b, g, i: (b, g, 0, i)),
            cmp_spec, cmp_spec,
            pl.BlockSpec((None, None, seq, 2 * HEAD_DIM), lambda b, g, i: (b, g, 0, 0)),
            kv_spec(0), kv_spec(1), kv_spec(2),
            pl.BlockSpec((tq, LANES), lambda b, g, i: (b * nq + i, g)),
        ],
        out_specs=pl.BlockSpec((tq, GQA * HEAD_DIM), lambda b, g, i: (b * nq + i, g)),
        out_shape=jax.ShapeDtypeStruct((batch * seq, N_HEADS * HEAD_DIM), BF),
        scratch_shapes=[
            pltpu.VMEM((nb, tq), F32),
            pltpu.VMEM((2, 1, rq), F32),
            pltpu.VMEM((2, 1, rq), F32),
            pltpu.VMEM((2, HEAD_DIM, rq), F32),
        ],
        compiler_params=_params("arbitrary", "arbitrary", "arbitrary"),
        name="nsa_prompt",
    )(qt, kc, vc, ksa, kv, kv, kv, gl)


def _cmp_means_kernel(pt_ref, *refs, n_pages):
    page_refs = refs[:n_pages]
    kc_ref, vc_ref = refs[n_pages], refs[n_pages + 1]
    bpp = PAGE_SIZE // NSA_BLOCK
    kvw = KV_HEADS * HEAD_DIM
    for t, out in enumerate((kc_ref, vc_ref)):
        x = jnp.concatenate([page_refs[p][t].reshape(kvw, PAGE_SIZE).T for p in range(n_pages)], axis=0)
        out[...] = jnp.sum(x.reshape(n_pages * bpp, NSA_BLOCK, kvw), axis=1) * (1.0 / NSA_BLOCK)


def _cmp_means(cache_t, layer, page_table, n_pages):
    batch, ppb = page_table.shape
    kvw = KV_HEADS * HEAD_DIM
    bpp = PAGE_SIZE // NSA_BLOCK
    steps = ppb // n_pages

    def page_spec(p):
        return pl.BlockSpec((None, None, 2, KV_HEADS, HEAD_DIM, PAGE_SIZE),
                            lambda b, s, pt: (layer, pt[b, s * n_pages + p], 0, 0, 0, 0))

    out_spec = pl.BlockSpec((None, n_pages * bpp, kvw), lambda b, s, pt: (b, s, 0))
    return pl.pallas_call(
        functools.partial(_cmp_means_kernel, n_pages=n_pages),
        grid_spec=pltpu.PrefetchScalarGridSpec(
            num_scalar_prefetch=1, grid=(batch, steps),
            in_specs=[page_spec(p) for p in range(n_pages)],
            out_specs=[out_spec, out_spec]),
        out_shape=[jax.ShapeDtypeStruct((batch, ppb * bpp, kvw), F32)] * 2,
        compiler_params=_params("arbitrary", "arbitrary"),
        name="cmp_means",
    )(page_table, *([cache_t] * n_pages))


def _nsa_sample_kernel(pt_ref, *refs, n_pages, past, s_new):
    (q_ref, kc_ref, vc_ref, new_ref, wnew_ref, wbuf_ref, gl_ref) = refs[:7]
    page_refs = refs[7:7 + n_pages]
    o_ref = refs[7 + n_pages]
    kcf, vcf, imp_ref, sel_ref, m_ref, l_ref, acc_ref, ocmp_ref = refs[8 + n_pages:]
    step = pl.program_id(1)
    nsteps = pl.num_programs(1)
    kvw = KV_HEADS * HEAD_DIM
    rq = GQA * s_new
    nbp = past // NSA_BLOCK
    nbf = kcf.shape[0]
    wlen = wbuf_ref.shape[-1]
    pad_rows = LANES - s_new

    def q_of(g):
        return q_ref[g * GQA:(g + 1) * GQA].reshape(rq, HEAD_DIM).astype(BF)

    def pad_keys(x):
        return jnp.concatenate([x, jnp.zeros((pad_rows, HEAD_DIM), F32)], axis=0).astype(BF)

    @pl.when(step == 0)
    def _():
        row8 = lax.broadcasted_iota(jnp.int32, (nbf - nbp, kvw), 0)
        for full, src, c0 in ((kcf, kc_ref, 0), (vcf, vc_ref, kvw)):
            full[0:nbp, :] = src[...]
            mean_new = jnp.sum(new_ref[:, c0:c0 + kvw], axis=0, keepdims=True) * (1.0 / NSA_BLOCK)
            full[nbp:nbf, :] = jnp.where(row8 == 0, mean_new, 0.0)
        col = lax.broadcasted_iota(jnp.int32, (nbf, rq), 1)
        qpos_q = past + lax.broadcasted_iota(jnp.int32, (nbf, s_new), 1)
        imps = []
        for g in range(KV_HEADS):
            lanes = slice(g * HEAD_DIM, (g + 1) * HEAD_DIM)
            o_cmp, imp = _cmp_branch(kcf[:, lanes], vcf[:, lanes], q_of(g), past + (col & (s_new - 1)), s_new)
            ocmp_ref[g] = o_cmp
            imps.append(_importance(imp, qpos_q))
        width = KV_HEADS * s_new
        imp_ref[...] = jnp.concatenate(imps * (LANES // width), axis=1)
        sel = _select_topk_packed(imp_ref, nbf, width)
        for g in range(KV_HEADS):
            sel_ref[g] = jnp.concatenate([sel[:, g * s_new:(g + 1) * s_new]] * GQA, axis=1)
        m_ref[...] = jnp.full(m_ref.shape, NEG, F32)
        l_ref[...] = jnp.zeros(l_ref.shape, F32)
        acc_ref[...] = jnp.zeros(acc_ref.shape, F32)

    nk = n_pages * PAGE_SIZE
    nbl = nk // NSA_BLOCK
    for g in range(KV_HEADS):
        kt = jnp.concatenate([page_refs[p][0, g] for p in range(n_pages)], axis=1).astype(BF)
        vt = jnp.concatenate([page_refs[p][1, g] for p in range(n_pages)], axis=1).astype(BF)
        sel_rows = sel_ref[g, pl.ds(pl.multiple_of(step * nbl, nbl), nbl), :]
        valid = _expand_blocks(sel_rows, nk, 0) > 0.5
        s = jnp.dot(q_of(g), kt, preferred_element_type=F32)
        _online_update(s, valid, vt, m_ref, l_ref, acc_ref, g)

    @pl.when(step == nsteps - 1)
    def _():
        tq_col = lax.broadcasted_iota(jnp.int32, (rq, LANES), 0) & (s_new - 1)
        tk = lax.broadcasted_iota(jnp.int32, (rq, LANES), 1)
        new_ok = (tk <= tq_col) & (tk < s_new)
        gate = _sigmoid(gl_ref[...])
        tail = nbf - 16
        for g in range(KV_HEADS):
            q = q_of(g)
            lane0 = 2 * kvw + g * HEAD_DIM
            k_new = pad_keys(new_ref[:, lane0:lane0 + HEAD_DIM])
            v_new = pad_keys(new_ref[:, lane0 + kvw:lane0 + kvw + HEAD_DIM])
            kb = lax.broadcasted_iota(jnp.int32, (16, LANES), 0)
            e = jnp.where(kb == nbp - tail, 1.0, 0.0).astype(BF)
            selx = lax.dot_general(sel_ref[g, tail:nbf, :].astype(BF), e, (((0,), (0,)), ((), ())),
                                   preferred_element_type=F32)
            s = lax.dot_general(q, k_new, (((1,), (1,)), ((), ())), preferred_element_type=F32)
            _online_update(s, (selx > 0.5) & new_ok, None, m_ref, l_ref, acc_ref, g, v_rows=v_new)
            o_sel = acc_ref[g] / l_ref[g][:, :HEAD_DIM]
            wl0 = g * HEAD_DIM
            kw_new = pad_keys(wnew_ref[:, wl0:wl0 + HEAD_DIM])
            vw_new = pad_keys(wnew_ref[:, kvw + wl0:kvw + wl0 + HEAD_DIM])
            sb = jnp.dot(q, wbuf_ref[0, g].astype(BF), preferred_element_type=F32)
            sn = lax.dot_general(q, kw_new, (((1,), (1,)), ((), ())), preferred_element_type=F32)
            jb = lax.broadcasted_iota(jnp.int32, (rq, wlen), 1)
            tq_b = lax.broadcasted_iota(jnp.int32, (rq, wlen), 0) & (s_new - 1)
            ok_b = (past - wlen + jb > past + tq_b - WINDOW) & (past - wlen + jb >= 0)
            sb = jnp.where(ok_b, sb, NEG)
            sn = jnp.where(new_ok, sn, NEG)
            mx = jnp.maximum(jnp.max(sb, axis=1, keepdims=True), jnp.max(sn, axis=1, keepdims=True))
            pb = jnp.where(ok_b, jnp.exp2(sb - mx), 0.0)
            pn = jnp.where(new_ok, jnp.exp2(sn - mx), 0.0)
            den = jnp.sum(pb, axis=1, keepdims=True) + jnp.sum(pn, axis=1, keepdims=True)
            o_win = (lax.dot_general(pb.astype(BF), wbuf_ref[1, g].astype(BF), (((1,), (1,)), ((), ())),
                                     preferred_element_type=F32)
                     + jnp.dot(pn.astype(BF), vw_new, preferred_element_type=F32)) / den
            o_cmp = ocmp_ref[g]
            for r in range(GQA):
                rows = slice(r * s_new, (r + 1) * s_new)
                c = g * LANES + 3 * r
                h = g * GQA + r
                o_ref[:, h * HEAD_DIM:(h + 1) * HEAD_DIM] = (
                    gate[:, c:c + 1] * o_cmp[rows] + gate[:, c + 1:c + 2] * o_sel[rows]
                    + gate[:, c + 2:c + 3] * o_win[rows]).astype(o_ref.dtype)


def _nsa_sample(cache_t, layer, page_table, q, kc, vc, nsa_new, win_new, wbuf_t, gl, n_pages, past, s_new):
    batch, ppb = page_table.shape
    kvw = KV_HEADS * HEAD_DIM
    qw = N_HEADS * HEAD_DIM
    nbp = past // NSA_BLOCK
    nbf = nbp + 8
    groups = LANES // (KV_HEADS * s_new)
    assert groups * KV_HEADS * s_new == LANES and nbf % groups == 0 and (nbf // groups) % 6 == 0
    wlen = wbuf_t.shape[-1]
    rq = GQA * s_new
    steps = ppb // n_pages

    def page_spec(p):
        return pl.BlockSpec((None, None, 2, KV_HEADS, HEAD_DIM, PAGE_SIZE),
                            lambda b, s, pt: (layer, pt[b, s * n_pages + p], 1, 0, 0, 0))

    per_b = lambda shape: pl.BlockSpec((None,) + shape, lambda b, s, pt: (b,) + (0,) * len(shape))
    rows_b = lambda w: pl.BlockSpec((s_new, w), lambda b, s, pt: (b, 0))
    return pl.pallas_call(
        functools.partial(_nsa_sample_kernel, n_pages=n_pages, past=past, s_new=s_new),
        grid_spec=pltpu.PrefetchScalarGridSpec(
            num_scalar_prefetch=1, grid=(batch, steps),
            in_specs=[
                per_b((N_HEADS, s_new, HEAD_DIM)),
                per_b((nbp, kvw)), per_b((nbp, kvw)),
                rows_b(4 * kvw), rows_b(2 * kvw),
                pl.BlockSpec((None, None, 2, KV_HEADS, HEAD_DIM, wlen), lambda b, s, pt: (layer, b, 0, 0, 0, 0)),
                rows_b(KV_HEADS * LANES),
            ] + [page_spec(p) for p in range(n_pages)],
            out_specs=rows_b(qw),
            scratch_shapes=[
                pltpu.VMEM((nbf, kvw), F32), pltpu.VMEM((nbf, kvw), F32),
                pltpu.VMEM((nbf, LANES), F32),
                pltpu.VMEM((KV_HEADS, nbf, rq), F32),
                pltpu.VMEM((KV_HEADS, rq, LANES), F32),
                pltpu.VMEM((KV_HEADS, rq, LANES), F32),
                pltpu.VMEM((KV_HEADS, rq, HEAD_DIM), F32),
                pltpu.VMEM((KV_HEADS, rq, HEAD_DIM), F32),
            ]),
        out_shape=jax.ShapeDtypeStruct((batch * s_new, qw), BF),
        compiler_params=_params("arbitrary", "arbitrary"),
        name="nsa_sample",
    )(page_table, q, kc, vc, nsa_new, win_new, wbuf_t, gl, *([cache_t] * n_pages))


def _rope_tables(pos):
    half = HEAD_DIM // 2
    freq = ROPE_THETA ** (-jnp.arange(half, dtype=F32) / half)
    ang = pos.astype(F32)[:, None] * freq[None, :]
    cos, sin = jnp.cos(ang), jnp.sin(ang)
    return jnp.tile(cos, (1, LANES // half)), jnp.tile(jnp.concatenate([-sin, sin], axis=1), (1, LANES // HEAD_DIM))


def _trunk(groups, mod4, mod5, wts):
    (g_mix, g_ffn, g_final, w_in_even_t, w_gate_t, sconv_w, w_out_even, w_in_odd, cconv_w, cconv_b, c_ln_g,
     c_ln_b, d_ln_g, d_ln_b, d_ws, d_bs_t, w_out_odd, w_up, ffn_conv_w, w_down) = wts
    prompt, sample = groups
    depth = g_mix.shape[0]
    d = prompt["x"].shape[1]
    sp = prompt["seq"]
    a_w = sconv_w.shape[-1]
    dff = ffn_conv_w.shape[-1]
    kv_cols = 3 * a_w + N_HEADS * HEAD_DIM + 6 * KV_HEADS * HEAD_DIM
    mm_tm = 1024
    down_tm = 512
    out_tn = 1024

    def prompt_gate(l, which, tm, tn):
        return pl.BlockSpec((None, None, None, 1, tn), lambda j, i: (l, which, (i * tm) // sp, 0, j))

    def sample_gate(l, which):
        b0 = sample["b_off"]
        return jnp.repeat(mod4[l, which, b0:b0 + sample["batch"]], sample["seq"], axis=0)

    def norm(xs, g, l, which):
        return [_norm_mod(x, g, mod5, l, which, grp["b_off"], grp["norm_tm"], grp["seq"] // grp["norm_tm"])
                for x, grp in zip(xs, groups)]

    xs = [prompt["x"], sample["x"]]
    new = [dict(nsa=[], win=[], s=[], c=[], dv=[], f=[]) for _ in groups]
    for l in range(depth):
        i = l // 2
        h = norm(xs, g_mix, l, 0)
        if l % 2 == 0:
            zs = _mm([h[0]], w_in_even_t, i, kv_cols, mm_tm, kv_cols // 4, wt=True, second=([h[1]], None, None),
                     name="in_even")
            gls = _mm([h[0]], w_gate_t, i, KV_HEADS * LANES, mm_tm, 512, wt=True, second=([h[1]], None, None),
                      name="gate_logits")
            mixed = []
            for z, gl, grp, out in zip(zs, gls, groups, new):
                mix_a, sb = _mixer_a(z, sconv_w, i, grp["sconv"][i], grp["tm"], grp["tpb"])
                o_b, nsa_rows, win_state = grp["nsa"](i, z, gl)
                mixed.append([mix_a, o_b])
                out["s"].append(sb)
                out["nsa"].append(nsa_rows)
                out["win"].append(win_state)
            w_out = w_out_even
        else:
            zs = _mm([h[0]], w_in_odd, i, w_in_odd.shape[-1], mm_tm, out_tn, second=([h[1]], None, None),
                     name="in_odd")
            mixed = []
            for z, grp, out in zip(zs, groups, new):
                mix, cb, v = _odd_post(z, i, grp["cconv"][i], cconv_w, cconv_b, c_ln_g, c_ln_b, d_ln_g, d_ln_b,
                                       d_ws, d_bs_t, grp["tm"], grp["tpb"], grp is sample)
                mixed.append([mix])
                out["c"].append(cb)
                out["dv"].append(v)
            w_out = w_out_odd
        xs = list(_mm(mixed[0], w_out, i, d, mm_tm, out_tn, res=xs[0], gate_spec=prompt_gate(l, 2, mm_tm, out_tn),
                      gate=mod5, second=(mixed[1], xs[1], sample_gate(l, 2)), name="out_proj"))
        h = norm(xs, g_ffn, l, 3)
        act_p, fb_p, a_s, g_s = _ffn_up(h[0], w_up, ffn_conv_w, l, prompt["ffn"][l], mm_tm, sp // mm_tm, 512, h[1])
        act_s, fb_s = _ffn_act(a_s, g_s, ffn_conv_w, l, sample["ffn"][l], sample["tm"], sample["tpb"], dff)
        new[0]["f"].append(fb_p)
        new[1]["f"].append(fb_s)
        xs = list(_mm([act_p], w_down, l, d, down_tm, 512, res=xs[0], gate_spec=prompt_gate(l, 5, down_tm, 512),
                      gate=mod5, second=([act_s], xs[1], sample_gate(l, 5)), name="ffn_down"))
    ys = [_final_norm(x, g_final, grp["norm_tm"]) for x, grp in zip(xs, groups)]
    return ys, new


def kernel(x_prompt, x_sample, cache_nsa_kv, state_win_kv, state_sconv, state_cconv, state_ffn_conv, page_table, c_prompt, c_sample, g_mix, g_ffn, g_final, w_ada, b_ada, w_in_even, sconv_w, w_out_even, w_in_odd, cconv_w, cconv_b, c_ln_g, c_ln_b, d_ln_g, d_ln_b, d_ws, d_bs, w_out_odd, w_up, ffn_conv_w, w_down):
    bp, sp, d = x_prompt.shape
    bs, ss, _ = x_sample.shape
    depth = g_mix.shape[0]
    n_even = w_in_even.shape[0]
    past = page_table.shape[1] * PAGE_SIZE
    a_w = sconv_w.shape[-1]
    dff = ffn_conv_w.shape[-1]
    kvw = KV_HEADS * HEAD_DIM
    dt = x_prompt.dtype

    rows = -(-(bp + bs) // 8) * 8
    c_all = jnp.concatenate([c_prompt, c_sample, jnp.zeros((rows - bp - bs, d), dt)], axis=0)
    mod4 = _ada(c_all, w_ada, b_ada)
    mod5 = mod4.reshape(depth, 6, rows, 1, d)

    gate_c0 = 3 * a_w + N_HEADS * HEAD_DIM + 6 * kvw
    w_in_even_t = jnp.swapaxes(w_in_even, 1, 2)
    wg = w_in_even_t[:, gate_c0:, :].reshape(n_even, KV_HEADS, GQA * 3, d)
    w_gate_t = jnp.pad(wg, ((0, 0), (0, 0), (0, LANES - GQA * 3), (0, 0))).reshape(n_even, KV_HEADS * LANES, d)

    wts = (g_mix, g_ffn, g_final, w_in_even_t, w_gate_t, sconv_w, w_out_even, w_in_odd, cconv_w, cconv_b, c_ln_g,
           c_ln_b, d_ln_g, d_ln_b, d_ws, jnp.swapaxes(d_bs, 1, 2), w_out_odd, w_up, ffn_conv_w, w_down)

    tm_p = 512
    tq = 256
    cos_p, sin_p = _rope_tables(jnp.arange(sp, dtype=jnp.int32))

    def prompt_nsa(i, z, gl):
        qt, nsat, wint, kv, ksa, kc, vc = _rope_prompt(z, cos_p, sin_p, bp, sp, tm_p)
        o_b = _nsa_prompt(qt, kc, vc, ksa, kv, gl, bp, sp, tq)
        keep = min(WINDOW, sp)
        nsa_rows = jnp.transpose(nsat, (0, 4, 1, 2, 3))
        win_state = jnp.transpose(wint[..., sp - keep:], (0, 4, 1, 2, 3))
        return o_b, nsa_rows, win_state

    zeros = lambda n, r, w: jnp.zeros((n, bp, r, w), dt)
    prompt = dict(x=x_prompt.reshape(bp * sp, d), b_off=0, batch=bp, seq=sp, tm=tm_p, tpb=sp // tm_p, norm_tm=2 * tm_p,
                  sconv=zeros(n_even, 2, a_w), cconv=zeros(depth // 2, CCONV_W - 1, a_w), ffn=zeros(depth, 2, dff),
                  nsa=prompt_nsa)

    cache_t = jnp.transpose(cache_nsa_kv, (0, 1, 3, 4, 5, 2))
    wbuf_t = jnp.transpose(state_win_kv, (0, 1, 3, 4, 5, 2))
    cos_s, sin_s = _rope_tables(past + jnp.arange(ss, dtype=jnp.int32))

    def sample_nsa(i, z, gl):
        q, nsa_new, win_new = _rope_sample(z, cos_s, sin_s, bs, ss)
        kc, vc = _cmp_means(cache_t, i, page_table, 32)
        o_b = _nsa_sample(cache_t, i, page_table, q, kc, vc, nsa_new, win_new, wbuf_t, gl, 32, past, ss)
        win_new_t = jnp.transpose(win_new.reshape(bs, ss, 2, KV_HEADS, HEAD_DIM), (0, 2, 3, 4, 1))
        win_t = jnp.concatenate([wbuf_t[i], win_new_t], axis=-1)[..., ss:]
        win_state = jnp.transpose(win_t, (0, 4, 1, 2, 3))
        return o_b, nsa_new.reshape(bs, ss, 4, KV_HEADS, HEAD_DIM), win_state

    sample = dict(x=x_sample.reshape(bs * ss, d), b_off=bp, batch=bs, seq=ss, tm=ss, tpb=1, norm_tm=ss,
                  sconv=state_sconv, cconv=state_cconv, ffn=state_ffn_conv, nsa=sample_nsa)

    (y_p, y_s), (new_p, new_s) = _trunk((prompt, sample), mod4, mod5, wts)

    nsa_p = [a.reshape(bp, sp, 4, KV_HEADS, HEAD_DIM) for a in new_p["nsa"]]
    win_p = [a.reshape(bp, -1, 2, KV_HEADS, HEAD_DIM) for a in new_p["win"]]
    dv_s = [a.reshape(bs, ss, -1) for a in new_s["dv"]]
    return (y_p.reshape(bp, sp, d), y_s.reshape(bs, ss, d), jnp.stack(nsa_p), jnp.stack(new_s["nsa"]),
            jnp.stack(win_p), jnp.stack(new_s["win"]), jnp.stack(new_p["s"]), jnp.stack(new_s["s"]),
            jnp.stack(new_p["c"]), jnp.stack(new_s["c"]), jnp.stack(dv_s), jnp.stack(new_p["f"]),
            jnp.stack(new_s["f"]))
```
